```python
import math
import jax, jax.numpy as jnp
from jax import lax
import numpy as np

D_MODEL = 2048
BATCH = 2
SEQ = 8192
DEPTH = 4
DEC_BATCH = 8
DEC_SEQ = 4096
PAST_LEN = 128

HEAD_DIM = 128
SCALE = HEAD_DIM ** -0.5
A_HEADS = 6
A_KV_HEADS = 2
B_HEADS = 4
B_KV_HEADS = 2
B_WINDOW = 128
C_PATTERNS = ((128, 1), (512, 4), (2048, 16))
C_GROUPS = len(C_PATTERNS)
C_HEADS_PER_GROUP = 2
C_HEADS = C_GROUPS * C_HEADS_PER_GROUP
C_KV_HEADS = C_GROUPS
MIX_WIDTH = (A_HEADS + B_HEADS + C_HEADS) * HEAD_DIM
Q_BLOCK = 128
GRID_W = 64
ROPE_THETA = 10000.0
ROPE_AXIS_DIM = HEAD_DIM // 2
REL_BUCKETS = 32
REL_MAX_DIST = 1024
REL_HEADS = B_HEADS + C_HEADS
MEM_LEN = 256
X_HEADS = 4
X_WIDTH = X_HEADS * HEAD_DIM
D_FF = ((8 * D_MODEL + 3 * 256 - 1) // (3 * 256)) * 256
RMS_EPS = 1e-6
NEG_INF = -1e30
PROJ_SIZES = (A_HEADS * HEAD_DIM, A_KV_HEADS * HEAD_DIM, A_KV_HEADS * HEAD_DIM,
              B_HEADS * HEAD_DIM, B_KV_HEADS * HEAD_DIM, B_KV_HEADS * HEAD_DIM,
              C_HEADS * HEAD_DIM, C_KV_HEADS * HEAD_DIM, C_KV_HEADS * HEAD_DIM)
PROJ_WIDTH = sum(PROJ_SIZES)

kernel_name = 'hybrid_parallel_encoder'


def rmsnorm(x, g):
    xf = x.astype(jnp.float32)
    y = xf * lax.rsqrt(jnp.mean(xf * xf, axis=-1, keepdims=True) + RMS_EPS)
    return (y * g.astype(jnp.float32)).astype(x.dtype)


def axial_rope_tables(seq_len):
    rows = seq_len // GRID_W
    row = jnp.repeat(jnp.arange(rows), GRID_W).astype(jnp.float32)
    col = jnp.tile(jnp.arange(GRID_W), rows).astype(jnp.float32)
    inv = ROPE_THETA ** (-jnp.arange(0, ROPE_AXIS_DIM, 2, dtype=jnp.float32) / ROPE_AXIS_DIM)
    ang_r = row[:, None] * inv
    ang_c = col[:, None] * inv
    return (jnp.cos(ang_r), jnp.sin(ang_r), jnp.cos(ang_c), jnp.sin(ang_c))


def _rotate(u, c, s):
    u1, u2 = jnp.split(u, 2, axis=-1)
    c = c[:, None]
    s = s[:, None]
    return jnp.concatenate([u1 * c - u2 * s, u1 * s + u2 * c], axis=-1)


def apply_axial_rope(x, cos_r, sin_r, cos_c, sin_c):
    xf = x.astype(jnp.float32)
    xr, xc = jnp.split(xf, 2, axis=-1)
    out = jnp.concatenate([_rotate(xr, cos_r, sin_r), _rotate(xc, cos_c, sin_c)], axis=-1)
    return out.astype(x.dtype)


def t5_bucket(rel):
    nb = REL_BUCKETS // 2
    max_exact = nb // 2
    ret = jnp.where(rel > 0, nb, 0)
    n = jnp.abs(rel)
    large = max_exact + (jnp.log(jnp.maximum(n, 1).astype(jnp.float32) / max_exact)
                         / math.log(REL_MAX_DIST / max_exact) * (nb - max_exact)).astype(jnp.int32)
    large = jnp.minimum(large, nb - 1)
    return ret + jnp.where(n < max_exact, n, large)


def position_bias(rel_bias, rel, col0, n_heads):
    table = rel_bias[:, col0:col0 + n_heads].astype(jnp.float32)
    return jnp.moveaxis(table[t5_bucket(rel)], -1, 0)


def band_offsets(blk):
    return (jnp.arange(3 * blk) - blk)[None, :] - jnp.arange(blk)[:, None]


def to_residue(t, d):
    bt, s = t.shape[0], t.shape[1]
    rest = t.shape[2:]
    return jnp.moveaxis(t.reshape(bt, s // d, d, *rest), 2, 1).reshape(bt * d, s // d, *rest)


def from_residue(t, d, bt):
    l = t.shape[1]
    rest = t.shape[2:]
    return jnp.moveaxis(t.reshape(bt, d, l, *rest), 1, 2).reshape(bt, l * d, *rest)


def dense_attention_blocks(q, k, v):
    bt, s, h, d = q.shape
    hkv = k.shape[2]
    g = h // hkv
    nb = s // Q_BLOCK
    qb = jnp.moveaxis(q.reshape(bt, nb, Q_BLOCK, hkv, g, d), 1, 0)

    def attend(qi):
        sc = jnp.einsum('bqkgd,bskd->bkgqs', qi, k, preferred_element_type=jnp.float32)
        p = jax.nn.softmax(sc, axis=-1).astype(v.dtype)
        return jnp.einsum('bkgqs,bskd->bqkgd', p, v, preferred_element_type=jnp.float32).astype(q.dtype)

    o = lax.map(attend, qb)
    return jnp.moveaxis(o, 0, 1).reshape(bt, s, h, d)


def banded_attention(q, k, v, half_window, bias, sink):
    bt, l, h, d = q.shape
    hkv = k.shape[2]
    g = h // hkv
    blk = half_window
    nb = -(-l // blk)
    pad = nb * blk - l
    qb = jnp.pad(q, ((0, 0), (0, pad), (0, 0), (0, 0))).reshape(bt, nb, blk, hkv, g, d)

    def windows(t):
        tp = jnp.pad(t, ((0, 0), (blk, blk + pad), (0, 0), (0, 0))).reshape(bt, nb + 2, blk, hkv, d)
        return jnp.concatenate([tp[:, :-2], tp[:, 1:-1], tp[:, 2:]], axis=2)

    kw = windows(k)
    vw = windows(v)
    sc = jnp.einsum('bnqkgd,bnskd->bnkgqs', qb, kw, preferred_element_type=jnp.float32)
    off = band_offsets(blk)
    key_pos = jnp.arange(nb)[:, None, None] * blk + (jnp.arange(3 * blk) - blk)[None, None, :]
    valid = (jnp.abs(off) <= half_window)[None] & (key_pos >= 0) & (key_pos < l)
    sc = sc + bias.astype(jnp.float32).reshape(hkv, g, blk, 3 * blk)
    sc = jnp.where(valid[None, :, None, None], sc, NEG_INF)
    m = jnp.max(sc, axis=-1, keepdims=True)
    if sink is not None:
        sk = sink.astype(jnp.float32).reshape(hkv, g, 1, 1)
        m = jnp.maximum(m, sk)
    e = jnp.exp(sc - m)
    den = jnp.sum(e, axis=-1, keepdims=True)
    if sink is not None:
        den = den + jnp.exp(sk - m)
    o = jnp.einsum('bnkgqs,bnskd->bnqkgd', (e / den).astype(v.dtype), vw,
                   preferred_element_type=jnp.float32)
    lse = (m + jnp.log(den))[..., 0]
    o = o.reshape(bt, nb * blk, h, d)[:, :l].astype(q.dtype)
    lse = jnp.moveaxis(lse, -1, 2).reshape(bt, nb * blk, h)[:, :l]
    return o, lse


def hybrid_mixer(h, w_in, q_gain, k_gain, sink, w_out, rope, bias_b, bias_c):
    bt, s, _ = h.shape
    splits = np.cumsum(PROJ_SIZES)[:-1]
    qa, ka, va, qb, kb, vb, qc, kc, vc = jnp.split(h @ w_in, splits, axis=-1)

    def heads(t, n):
        return t.reshape(bt, s, n, HEAD_DIM)

    qa = apply_axial_rope(rmsnorm(heads(qa, A_HEADS), q_gain), *rope) * SCALE
    ka = apply_axial_rope(rmsnorm(heads(ka, A_KV_HEADS), k_gain), *rope)
    out_a = dense_attention_blocks(qa, ka, heads(va, A_KV_HEADS))

    out_b, _ = banded_attention(heads(qb, B_HEADS) * SCALE, heads(kb, B_KV_HEADS), heads(vb, B_KV_HEADS),
                                B_WINDOW, bias_b, sink)

    qc = heads(qc, C_HEADS) * SCALE
    kc = heads(kc, C_KV_HEADS)
    vc = heads(vc, C_KV_HEADS)
    outs = []
    lses = []
    for gi, (window, dil) in enumerate(C_PATTERNS):
        hs = slice(gi * C_HEADS_PER_GROUP, (gi + 1) * C_HEADS_PER_GROUP)
        o_g, lse_g = banded_attention(to_residue(qc[:, :, hs], dil), to_residue(kc[:, :, gi:gi + 1], dil),
                                      to_residue(vc[:, :, gi:gi + 1], dil), window // (2 * dil),
                                      bias_c[gi], None)
        outs.append(from_residue(o_g, dil, bt))
        lses.append(from_residue(lse_g, dil, bt))
    alpha = jax.nn.softmax(jnp.stack(lses, axis=2), axis=2)
    out_c = (jnp.stack(outs, axis=2).astype(jnp.float32) * alpha[..., None]).astype(h.dtype)

    mixed = jnp.concatenate([out_a.reshape(bt, s, -1), out_b.reshape(bt, s, -1),
                             out_c.reshape(bt, s, -1)], axis=-1)
    return mixed @ w_out


def memory_cross_attention(h, mem_h, w_cq, w_ckv, w_co):
    bt, s, _ = h.shape
    m = mem_h.shape[1]
    q = (h @ w_cq).reshape(bt, s, X_HEADS, HEAD_DIM) * SCALE
    kv = (mem_h @ w_ckv).reshape(bt, m, 2, X_HEADS, HEAD_DIM)
    sc = jnp.einsum('bshd,bmhd->bhsm', q, kv[:, :, 0], preferred_element_type=jnp.float32)
    p = jax.nn.softmax(sc, axis=-1).astype(kv.dtype)
    o = jnp.einsum('bhsm,bmhd->bshd', p, kv[:, :, 1], preferred_element_type=jnp.float32).astype(h.dtype)
    return o.reshape(bt, s, X_WIDTH) @ w_co


def swiglu(h, w_ffn_in, w_ffn_out):
    gate, up = jnp.split(h @ w_ffn_in, 2, axis=-1)
    return (jax.nn.silu(gate) * up) @ w_ffn_out


def encode(x, mem, ln_mix, w_in, q_norm_a, k_norm_a, sink_b, rel_bias, w_out, ln_cross, ln_mem,
           w_cq, w_ckv, w_co, ln_ffn, w_ffn_in, w_ffn_out, ln_final):
    s = x.shape[1]
    rope = axial_rope_tables(s)
    bias_b = position_bias(rel_bias, band_offsets(B_WINDOW), 0, B_HEADS)
    bias_c = [position_bias(rel_bias, band_offsets(w // (2 * d)) * d, B_HEADS + gi * C_HEADS_PER_GROUP,
                            C_HEADS_PER_GROUP) for gi, (w, d) in enumerate(C_PATTERNS)]
    for l in range(DEPTH):
        x = x + hybrid_mixer(rmsnorm(x, ln_mix[l]), w_in[l], q_norm_a[l], k_norm_a[l], sink_b[l], w_out[l],
                             rope, bias_b, bias_c)
        x = x + memory_cross_attention(rmsnorm(x, ln_cross[l]), rmsnorm(mem, ln_mem[l]),
                                       w_cq[l], w_ckv[l], w_co[l])
        x = x + swiglu(rmsnorm(x, ln_ffn[l]), w_ffn_in[l], w_ffn_out[l])
    return rmsnorm(x, ln_final)


def setup_inputs(seed: int = 0) -> dict:
    key = jax.random.key(seed)
    ks = jax.random.split(key, 24)

    def normal(k, shape, scale):
        return jax.random.normal(k, shape, jnp.float32) * scale

    def gain(k, shape):
        return 1.0 + normal(k, shape, 0.02)

    return {
        'x_prompt': normal(ks[0], (BATCH, SEQ, D_MODEL), 1.0),
        'x_sample': normal(ks[1], (DEC_BATCH, DEC_SEQ, D_MODEL), 1.0),
        'mem_prompt': normal(ks[2], (BATCH, MEM_LEN, D_MODEL), 1.0),
        'mem_sample': normal(ks[3], (DEC_BATCH, MEM_LEN, D_MODEL), 1.0),
        'ln_mix': gain(ks[4], (DEPTH, D_MODEL)),
        'w_in': normal(ks[5], (DEPTH, D_MODEL, PROJ_WIDTH), D_MODEL ** -0.5),
        'q_norm_a': gain(ks[6], (DEPTH, HEAD_DIM)),
        'k_norm_a': gain(ks[7], (DEPTH, HEAD_DIM)),
        'sink_b': normal(ks[8], (DEPTH, B_HEADS), 0.5),
        'rel_bias': normal(ks[9], (REL_BUCKETS, REL_HEADS), 0.5),
        'w_out': normal(ks[10], (DEPTH, MIX_WIDTH, D_MODEL), MIX_WIDTH ** -0.5),
        'ln_cross': gain(ks[11], (DEPTH, D_MODEL)),
        'ln_mem': gain(ks[12], (DEPTH, D_MODEL)),
        'w_cq': normal(ks[13], (DEPTH, D_MODEL, X_WIDTH), D_MODEL ** -0.5),
        'w_ckv': normal(ks[14], (DEPTH, D_MODEL, 2 * X_WIDTH), D_MODEL ** -0.5),
        'w_co': normal(ks[15], (DEPTH, X_WIDTH, D_MODEL), X_WIDTH ** -0.5),
        'ln_ffn': gain(ks[16], (DEPTH, D_MODEL)),
        'w_ffn_in': normal(ks[17], (DEPTH, D_MODEL, 2 * D_FF), D_MODEL ** -0.5),
        'w_ffn_out': normal(ks[18], (DEPTH, D_FF, D_MODEL), D_FF ** -0.5),
        'ln_final': gain(ks[19], (D_MODEL,)),
    }


def reference(x_prompt, x_sample, mem_prompt, mem_sample, ln_mix, w_in, q_norm_a, k_norm_a, sink_b, rel_bias,
              w_out, ln_cross, ln_mem, w_cq, w_ckv, w_co, ln_ffn, w_ffn_in, w_ffn_out, ln_final):
    y_prompt = encode(x_prompt, mem_prompt, ln_mix, w_in, q_norm_a, k_norm_a, sink_b, rel_bias, w_out,
                      ln_cross, ln_mem, w_cq, w_ckv, w_co, ln_ffn, w_ffn_in, w_ffn_out, ln_final)
    y_sample = encode(x_sample, mem_sample, ln_mix, w_in, q_norm_a, k_norm_a, sink_b, rel_bias, w_out,
                      ln_cross, ln_mem, w_cq, w_ckv, w_co, ln_ffn, w_ffn_in, w_ffn_out, ln_final)
    return (y_prompt, y_sample)
```

```python
import functools
import math

import jax
import jax.numpy as jnp
import numpy as np
from jax import lax
from jax.experimental import pallas as pl
from jax.experimental.pallas import tpu as pltpu

HEAD_DIM = 128
SCALE = HEAD_DIM ** -0.5
A_HEADS, A_KV_HEADS = 6, 2
B_HEADS, B_KV_HEADS = 4, 2
B_WINDOW = 128
C_PATTERNS = ((128, 1), (512, 4), (2048, 16))
C_HEADS_PER_GROUP = 2
C_GROUPS = len(C_PATTERNS)
GRID_W = 64
ROPE_THETA = 10000.0
ROPE_AXIS_DIM = HEAD_DIM // 2
REL_BUCKETS = 32
REL_MAX_DIST = 1024
X_HEADS = 4
RMS_EPS = 1e-6
NEG_INF = -1e30

_QA, _KA, _VA = 0, A_HEADS, A_HEADS + A_KV_HEADS
_QB = _VA + A_KV_HEADS
_KB, _VB = _QB + B_HEADS, _QB + B_HEADS + B_KV_HEADS
_QC = _VB + B_KV_HEADS
_KC = _QC + C_GROUPS * C_HEADS_PER_GROUP
_VC = _KC + C_GROUPS
PROJ_HEADS = _VC + C_GROUPS
_HEAD_TYPES = (("rope_q",) * A_HEADS + ("rope_k",) * A_KV_HEADS + ("plain",) * A_KV_HEADS
               + ("scale",) * B_HEADS + ("plain",) * (2 * B_KV_HEADS)
               + ("scale",) * (C_GROUPS * C_HEADS_PER_GROUP) + ("plain",) * (2 * C_GROUPS))
MIX_HEADS = A_HEADS + B_HEADS + C_GROUPS * C_HEADS_PER_GROUP

V7X_VMEM_LIMIT_BYTES = 56 * 1024 * 1024

BF16 = jnp.bfloat16
F32 = jnp.float32
_NT = (((1,), (1,)), ((), ()))


def _largest_tile(n, candidates):
    for c in candidates:
        if n % c == 0:
            return c
    raise ValueError(f"no tile in {candidates} divides {n}")


def _params(*sem):
    return pltpu.CompilerParams(dimension_semantics=sem, vmem_limit_bytes=V7X_VMEM_LIMIT_BYTES)


def _rms_normed(x, gain):
    return x * lax.rsqrt(jnp.mean(x * x, axis=-1, keepdims=True) + RMS_EPS) * gain


def _proj_kernel(*refs, head_types, heads_per_block):
    has_rope = any(t.startswith("rope") for t in head_types)
    if has_rope:
        x_ref, g_ref, w_ref, gains_ref, cos_ref, sin_ref, o_ref, xn_ref = refs
    else:
        x_ref, g_ref, w_ref, o_ref, xn_ref = refs
    j = pl.program_id(1)

    @pl.when(j == 0)
    def _():
        xn_ref[...] = _rms_normed(x_ref[...], g_ref[...]).astype(BF16)

    acc = jnp.dot(xn_ref[...], w_ref[...], preferred_element_type=F32)
    tm = acc.shape[0]

    def rope_head(y, gain, post):
        y = _rms_normed(y, gain)
        lane = lax.broadcasted_iota(jnp.int32, (tm, HEAD_DIM), 1)
        first_quarter = (lane % (HEAD_DIM // 2)) < (HEAD_DIM // 4)
        partner = jnp.where(first_quarter, pltpu.roll(y, 3 * HEAD_DIM // 4, 1), pltpu.roll(y, HEAD_DIM // 4, 1))
        out = y * cos_ref[...] + partner * sin_ref[...]
        return out * post if post != 1.0 else out

    def epilogue(types):
        h = 0
        while h < len(types):
            ty = types[h]
            if ty in ("rope_q", "rope_k"):
                sl = slice(h * HEAD_DIM, (h + 1) * HEAD_DIM)
                row = 0 if ty == "rope_q" else 1
                post = SCALE if ty == "rope_q" else 1.0
                o_ref[:, sl] = rope_head(acc[:, sl], gains_ref[row:row + 1, :], post).astype(BF16)
                h += 1
            else:
                e = h
                while e < len(types) and types[e] == ty:
                    e += 1
                sl = slice(h * HEAD_DIM, e * HEAD_DIM)
                val = acc[:, sl] * SCALE if ty == "scale" else acc[:, sl]
                o_ref[:, sl] = val.astype(BF16)
                h = e

    n_blocks = len(head_types) // heads_per_block
    if n_blocks == 1:
        epilogue(head_types)
    else:
        for jb in range(n_blocks):
            pl.when(j == jb)(functools.partial(
                epilogue, head_types[jb * heads_per_block:(jb + 1) * heads_per_block]))


def _proj(x2d, gain, w, head_types, *, seq_len, rope=None, tn_heads):
    m, d = x2d.shape
    n = w.shape[1]
    assert n == len(head_types) * HEAD_DIM and len(head_types) % tn_heads == 0
    tm = _largest_tile(math.gcd(m, seq_len), (1024, 512, 256))
    tn = tn_heads * HEAD_DIM
    in_specs = [pl.BlockSpec((tm, d), lambda i, j: (i, 0)),
                pl.BlockSpec((1, d), lambda i, j: (0, 0)),
                pl.BlockSpec((d, tn), lambda i, j: (0, j))]
    args = [x2d, gain.reshape(1, d), w]
    if rope is not None:
        gains, cos_t, sin_t = rope
        tiles_per_seq = seq_len // tm
        in_specs += [pl.BlockSpec((2, HEAD_DIM), lambda i, j: (0, 0)),
                     pl.BlockSpec((tm, HEAD_DIM), lambda i, j: (i % tiles_per_seq, 0)),
                     pl.BlockSpec((tm, HEAD_DIM), lambda i, j: (i % tiles_per_seq, 0))]
        args += [gains, cos_t, sin_t]
    return pl.pallas_call(
        functools.partial(_proj_kernel, head_types=tuple(head_types), heads_per_block=tn_heads),
        grid=(m // tm, n // tn),
        in_specs=in_specs,
        out_specs=pl.BlockSpec((tm, tn), lambda i, j: (i, j)),
        out_shape=jax.ShapeDtypeStruct((m, n), BF16),
        scratch_shapes=[pltpu.VMEM((tm, d), BF16)],
        compiler_params=_params("parallel", "arbitrary"),
        name="proj",
    )(*args)


def _attn_a_kernel(q_ref, k_ref, v_ref, o_ref, q_scr, m_scr, l_scr, acc_scr, *, group, tk):
    tq = q_ref.shape[0]
    seq = k_ref.shape[0]
    for h in range(group):
        q_scr[h * tq:(h + 1) * tq, :] = q_ref[:, h * HEAD_DIM:(h + 1) * HEAD_DIM]
    m_scr[...] = jnp.full(m_scr.shape, -jnp.inf, F32)
    l_scr[...] = jnp.zeros(l_scr.shape, F32)
    acc_scr[...] = jnp.zeros(acc_scr.shape, F32)

    def chunk(c, carry):
        off = pl.multiple_of(c * tk, tk)
        k = k_ref[pl.ds(off, tk), :]
        v = v_ref[pl.ds(off, tk), :]
        s = lax.dot_general(q_scr[...], k, _NT, preferred_element_type=F32)
        m_prev = m_scr[...]
        m_new = jnp.maximum(m_prev, jnp.max(s, axis=-1, keepdims=True))
        a = jnp.exp(m_prev - m_new)
        p = jnp.exp(s - m_new)
        l_scr[...] = a * l_scr[...] + jnp.sum(p, axis=-1, keepdims=True)
        acc_scr[...] = a * acc_scr[...] + jnp.dot(p.astype(BF16), v, preferred_element_type=F32)
        m_scr[...] = m_new
        return carry

    lax.fori_loop(0, seq // tk, chunk, 0)
    out = acc_scr[...] / l_scr[...]
    for h in range(group):
        o_ref[:, h * HEAD_DIM:(h + 1) * HEAD_DIM] = out[h * tq:(h + 1) * tq, :].astype(BF16)


def _attn_a(qkv, *, tq=256, tk=512):
    b, s, _ = qkv.shape
    group = A_HEADS // A_KV_HEADS
    gw = group * HEAD_DIM
    assert s % tq == 0 and s % tk == 0 and _QA == 0
    return pl.pallas_call(
        functools.partial(_attn_a_kernel, group=group, tk=tk),
        grid=(b, A_KV_HEADS, s // tq),
        in_specs=[pl.BlockSpec((None, tq, gw), lambda bi, kv, i: (bi, i, kv)),
                  pl.BlockSpec((None, s, HEAD_DIM), lambda bi, kv, i: (bi, 0, _KA + kv)),
                  pl.BlockSpec((None, s, HEAD_DIM), lambda bi, kv, i: (bi, 0, _VA + kv))],
        out_specs=pl.BlockSpec((None, tq, gw), lambda bi, kv, i: (bi, i, kv)),
        out_shape=jax.ShapeDtypeStruct((b, s, A_HEADS * HEAD_DIM), BF16),
        scratch_shapes=[pltpu.VMEM((group * tq, HEAD_DIM), BF16),
                        pltpu.VMEM((group * tq, 1), F32),
                        pltpu.VMEM((group * tq, 1), F32),
                        pltpu.VMEM((group * tq, HEAD_DIM), F32)],
        compiler_params=_params("parallel", "parallel", "arbitrary"),
        name="attn_a",
    )(qkv, qkv, qkv)


def _banded_kernel(*refs, blk, group, has_sink, emit_lse):
    q_ref, k_ref, v_ref, bias_ref = refs[:4]
    refs = refs[4:]
    sink_ref = None
    if has_sink:
        sink_ref, refs = refs[0], refs[1:]
    o_ref = refs[0]
    lse_ref = refs[1] if emit_lse else None
    tl = q_ref.shape[0]
    n_blocks = k_ref.shape[0] // blk
    blocks_per_tile = tl // blk
    t = pl.program_id(3)

    def block(i, carry):
        n = t * blocks_per_tile + i
        row0 = pl.multiple_of(i * blk, blk)
        start = pl.multiple_of(jnp.clip(n - 1, 0, n_blocks - 3) * blk, blk)
        variant = jnp.where(n == 0, 1, jnp.where(n == n_blocks - 1, 2, 0))
        q = jnp.concatenate([q_ref[pl.ds(row0, blk), h * HEAD_DIM:(h + 1) * HEAD_DIM] for h in range(group)], axis=0)
        kw = k_ref[pl.ds(start, 3 * blk), :]
        vw = v_ref[pl.ds(start, 3 * blk), :]
        s = lax.dot_general(q, kw, _NT, preferred_element_type=F32) + bias_ref[variant]
        m = jnp.max(s, axis=-1, keepdims=True)
        if has_sink:
            sk = sink_ref[...]
            m = jnp.maximum(m, sk)
        e = jnp.exp(s - m)
        den = jnp.sum(e, axis=-1, keepdims=True)
        if has_sink:
            den = den + jnp.exp(sk - m)
        o = jnp.dot(e.astype(BF16), vw, preferred_element_type=F32) / den
        for h in range(group):
            o_ref[pl.ds(row0, blk), h * HEAD_DIM:(h + 1) * HEAD_DIM] = o[h * blk:(h + 1) * blk, :].astype(BF16)
        if emit_lse:
            lse = m + jnp.log(den)
            for h in range(group):
                lse_ref[pl.ds(row0, blk), h * HEAD_DIM:(h + 1) * HEAD_DIM] = jnp.broadcast_to(
                    lse[h * blk:(h + 1) * blk, :], (blk, HEAD_DIM))
        return carry

    lax.fori_loop(0, blocks_per_tile, block, 0)


def _banded(qkv, bias, sink_col, *, dil, blk, group, kv_heads, q_head0, k_head0, v_head0, emit_lse):
    b, s, width = qkv.shape
    heads_per_row = width // HEAD_DIM
    length = s // dil
    assert s % dil == 0 and length % blk == 0 and length // blk >= 3
    tl = _largest_tile(length, (2048, 1024, 512, 256))
    gw = group * HEAD_DIM
    assert q_head0 % group == 0 and heads_per_row % group == 0
    qkv_r = qkv.reshape(b, length, dil * width)
    in_specs = [
        pl.BlockSpec((None, tl, gw), lambda bi, r, kv, t: (bi, t, (r * heads_per_row + q_head0) // group + kv)),
        pl.BlockSpec((None, length, HEAD_DIM), lambda bi, r, kv, t: (bi, 0, r * heads_per_row + k_head0 + kv)),
        pl.BlockSpec((None, length, HEAD_DIM), lambda bi, r, kv, t: (bi, 0, r * heads_per_row + v_head0 + kv)),
        pl.BlockSpec((None, 3, group * blk, 3 * blk), lambda bi, r, kv, t: (kv, 0, 0, 0)),
    ]
    args = [qkv_r, qkv_r, qkv_r, bias]
    if sink_col is not None:
        in_specs.append(pl.BlockSpec((None, group * blk, 1), lambda bi, r, kv, t: (kv, 0, 0)))
        args.append(sink_col)
    out_w = kv_heads * gw
    o_spec = pl.BlockSpec((None, tl, gw), lambda bi, r, kv, t: (bi, t, r * kv_heads + kv))
    out_shape = [jax.ShapeDtypeStruct((b, length, dil * out_w), BF16)]
    out_specs = [o_spec]
    if emit_lse:
        out_shape.append(jax.ShapeDtypeStruct((b, length, dil * out_w), F32))
        out_specs.append(o_spec)
    outs = pl.pallas_call(
        functools.partial(_banded_kernel, blk=blk, group=group, has_sink=sink_col is not None, emit_lse=emit_lse),
        grid=(b, dil, kv_heads, length // tl),
        in_specs=in_specs,
        out_specs=out_specs,
        out_shape=out_shape,
        compiler_params=_params("parallel", "parallel", "parallel", "arbitrary"),
        name=f"banded_d{dil}",
    )(*args)
    return [o.reshape(b, s, out_w) for o in outs]


def _out_proj_kernel(oa_ref, ob_ref, oc0_ref, oc1_ref, oc2_ref, l0_ref, l1_ref, l2_ref, x_ref, w_ref,
                     out_ref, lhs_ref):
    j = pl.program_id(1)

    @pl.when(j == 0)
    def _():
        a_w = oa_ref.shape[1]
        b_w = ob_ref.shape[1]
        c_w = oc0_ref.shape[1]
        lhs_ref[:, 0:a_w] = oa_ref[...]
        lhs_ref[:, a_w:a_w + b_w] = ob_ref[...]
        ls = [l0_ref[...], l1_ref[...], l2_ref[...]]
        mx = jnp.maximum(jnp.maximum(ls[0], ls[1]), ls[2])
        es = [jnp.exp(l - mx) for l in ls]
        tot = es[0] + es[1] + es[2]
        for gi, oc_ref in enumerate((oc0_ref, oc1_ref, oc2_ref)):
            c0 = a_w + b_w + gi * c_w
            lhs_ref[:, c0:c0 + c_w] = (oc_ref[...].astype(F32) * (es[gi] / tot)).astype(BF16)

    out_ref[...] = x_ref[...] + jnp.dot(lhs_ref[...], w_ref[...], preferred_element_type=F32)


def _out_proj(x2d, oa, ob, ocs, lses, w):
    m, d = x2d.shape
    k = w.shape[0]
    tm = _largest_tile(m, (1024, 512, 256))
    tn = _largest_tile(d, (512, 256, 128))
    row = lambda i, j: (i, 0)
    lhs = [oa, ob, *ocs, *lses]
    assert sum(a.shape[1] for a in (oa, ob, *ocs)) == k
    return pl.pallas_call(
        _out_proj_kernel,
        grid=(m // tm, d // tn),
        in_specs=[pl.BlockSpec((tm, a.shape[1]), row) for a in lhs]
        + [pl.BlockSpec((tm, tn), lambda i, j: (i, j)), pl.BlockSpec((k, tn), lambda i, j: (0, j))],
        out_specs=pl.BlockSpec((tm, tn), lambda i, j: (i, j)),
        out_shape=jax.ShapeDtypeStruct((m, d), F32),
        scratch_shapes=[pltpu.VMEM((tm, k), BF16)],
        compiler_params=_params("parallel", "arbitrary"),
        name="out_proj",
    )(*lhs, x2d, w)


def _cross_kernel(x_ref, g_ref, wq_ref, kv_ref, wo_ref, out_ref):
    x = x_ref[...]
    xn = _rms_normed(x, g_ref[...]).astype(BF16)
    q = (jnp.dot(xn, wq_ref[...], preferred_element_type=F32) * SCALE).astype(BF16)
    xw = X_HEADS * HEAD_DIM
    heads = []
    for h in range(X_HEADS):
        sl = slice(h * HEAD_DIM, (h + 1) * HEAD_DIM)
        k = kv_ref[:, sl]
        v = kv_ref[:, xw + h * HEAD_DIM:xw + (h + 1) * HEAD_DIM]
        s = lax.dot_general(q[:, sl], k, _NT, preferred_element_type=F32)
        e = jnp.exp(s - jnp.max(s, axis=-1, keepdims=True))
        den = jnp.sum(e, axis=-1, keepdims=True)
        heads.append((jnp.dot(e.astype(BF16), v, preferred_element_type=F32) / den).astype(BF16))
    o = jnp.concatenate(heads, axis=1)
    out_ref[...] = x + jnp.dot(o, wo_ref[...], preferred_element_type=F32)


def _cross(x2d, gain, wq, kv, wo, *, seq_len):
    m, d = x2d.shape
    tm = _largest_tile(math.gcd(m, seq_len), (512, 256))
    tiles_per_seq = seq_len // tm
    mem_len, kvw = kv.shape[1], kv.shape[2]
    return pl.pallas_call(
        _cross_kernel,
        grid=(m // tm,),
        in_specs=[pl.BlockSpec((tm, d), lambda i: (i, 0)),
                  pl.BlockSpec((1, d), lambda i: (0, 0)),
                  pl.BlockSpec(wq.shape, lambda i: (0, 0)),
                  pl.BlockSpec((None, mem_len, kvw), lambda i: (i // tiles_per_seq, 0, 0)),
                  pl.BlockSpec(wo.shape, lambda i: (0, 0))],
        out_specs=pl.BlockSpec((tm, d), lambda i: (i, 0)),
        out_shape=jax.ShapeDtypeStruct((m, d), F32),
        compiler_params=_params("parallel"),
        name="cross",
    )(x2d, gain.reshape(1, d), wq, kv, wo)


def _ffn_kernel(*refs, final_norm):
    if final_norm:
        x_ref, g_ref, wg_ref, wu_ref, wo_ref, gf_ref, out_ref, xn_ref = refs
    else:
        x_ref, g_ref, wg_ref, wu_ref, wo_ref, out_ref, xn_ref = refs
    c = pl.program_id(1)

    @pl.when(c == 0)
    def _():
        x = x_ref[...]
        xn_ref[...] = _rms_normed(x, g_ref[...]).astype(BF16)
        out_ref[...] = x

    h = xn_ref[...]
    gate = jnp.dot(h, wg_ref[...], preferred_element_type=F32)
    up = jnp.dot(h, wu_ref[...], preferred_element_type=F32)
    act = (gate * jax.nn.sigmoid(gate) * up).astype(BF16)
    out_ref[...] += jnp.dot(act, wo_ref[...], preferred_element_type=F32)

    if final_norm:
        @pl.when(c == pl.num_programs(1) - 1)
        def _():
            out_ref[...] = _rms_normed(out_ref[...], gf_ref[...])


def _ffn(x2d, gain, w_in, w_out, final_gain=None):
    m, d = x2d.shape
    d_ff = w_out.shape[0]
    assert w_in.shape == (d, 2 * d_ff)
    tm = _largest_tile(m, (512, 256))
    tf = _largest_tile(d_ff, (512, 256, 128))
    n_chunks = d_ff // tf
    in_specs = [pl.BlockSpec((tm, d), lambda i, c: (i, 0)),
                pl.BlockSpec((1, d), lambda i, c: (0, 0)),
                pl.BlockSpec((d, tf), lambda i, c: (0, c)),
                pl.BlockSpec((d, tf), lambda i, c: (0, c + n_chunks)),
                pl.BlockSpec((tf, d), lambda i, c: (c, 0))]
    args = [x2d, gain.reshape(1, d), w_in, w_in, w_out]
    if final_gain is not None:
        in_specs.append(pl.BlockSpec((1, d), lambda i, c: (0, 0)))
        args.append(final_gain.reshape(1, d))
    return pl.pallas_call(
        functools.partial(_ffn_kernel, final_norm=final_gain is not None),
        grid=(m // tm, n_chunks),
        in_specs=in_specs,
        out_specs=pl.BlockSpec((tm, d), lambda i, c: (i, 0)),
        out_shape=jax.ShapeDtypeStruct((m, d), F32),
        scratch_shapes=[pltpu.VMEM((tm, d), BF16)],
        compiler_params=_params("parallel", "arbitrary"),
        name="ffn",
    )(*args)


def _rope_tables(seq_len):
    rows = seq_len // GRID_W
    row = jnp.repeat(jnp.arange(rows), GRID_W).astype(F32)
    col = jnp.tile(jnp.arange(GRID_W), rows).astype(F32)
    inv = ROPE_THETA ** (-jnp.arange(0, ROPE_AXIS_DIM, 2, dtype=F32) / ROPE_AXIS_DIM)
    ang_r = row[:, None] * inv
    ang_c = col[:, None] * inv
    cos_t = jnp.concatenate([jnp.cos(ang_r)] * 2 + [jnp.cos(ang_c)] * 2, axis=-1)
    sin_t = jnp.concatenate([-jnp.sin(ang_r), jnp.sin(ang_r), -jnp.sin(ang_c), jnp.sin(ang_c)], axis=-1)
    return cos_t, sin_t


def _t5_bucket(rel):
    nb = REL_BUCKETS // 2
    max_exact = nb // 2
    ret = jnp.where(rel > 0, nb, 0)
    n = jnp.abs(rel)
    large = max_exact + (jnp.log(jnp.maximum(n, 1).astype(F32) / max_exact)
                         / math.log(REL_MAX_DIST / max_exact) * (nb - max_exact)).astype(jnp.int32)
    large = jnp.minimum(large, nb - 1)
    return ret + jnp.where(n < max_exact, n, large)


def _band_bias(rel_bias, head0, kv_heads, group, blk, dil):
    off = (jnp.arange(3 * blk) - blk)[None, :] - jnp.arange(blk)[:, None]
    table = rel_bias[:, head0:head0 + kv_heads * group].astype(F32)
    bias = jnp.moveaxis(table[_t5_bucket(off * dil)], -1, 0)
    bias = jnp.where((jnp.abs(off) <= blk)[None], bias, NEG_INF)
    masked = jnp.full((kv_heads * group, blk, blk), NEG_INF, F32)
    first = jnp.concatenate([bias[:, :, blk:], masked], axis=-1)
    last = jnp.concatenate([masked, bias[:, :, :2 * blk]], axis=-1)
    variants = jnp.stack([bias, first, last], axis=1)
    variants = variants.reshape(kv_heads, group, 3, blk, 3 * blk)
    return jnp.moveaxis(variants, 1, 2).reshape(kv_heads, 3, group * blk, 3 * blk)


def _encode(x, mem, ln_mix, w_in, q_norm_a, k_norm_a, sink_b, w_out, ln_cross, ln_mem, w_cq, w_ckv, w_co,
            ln_ffn, w_ffn_in, w_ffn_out, ln_final, bias_b, bias_c):
    b, s, d = x.shape
    mem_len = mem.shape[1]
    depth = w_in.shape[0]
    cos_t, sin_t = _rope_tables(s)
    x2d = x.reshape(b * s, d)
    mem2d = mem.reshape(b * mem_len, d)
    b_group = B_HEADS // B_KV_HEADS
    for l in range(depth):
        gains = jnp.stack([q_norm_a[l], k_norm_a[l]]).astype(F32)
        qkv = _proj(x2d, ln_mix[l], w_in[l], _HEAD_TYPES, seq_len=s, rope=(gains, cos_t, sin_t), tn_heads=10)
        qkv = qkv.reshape(b, s, PROJ_HEADS * HEAD_DIM)
        out_a = _attn_a(qkv)
        sink_col = jnp.repeat(sink_b[l].astype(F32).reshape(B_KV_HEADS, b_group), B_WINDOW, axis=1)[..., None]
        (out_b,) = _banded(qkv, bias_b, sink_col, dil=1, blk=B_WINDOW, group=b_group, kv_heads=B_KV_HEADS,
                           q_head0=_QB, k_head0=_KB, v_head0=_VB, emit_lse=False)
        ocs, lses = [], []
        for gi, (window, dil) in enumerate(C_PATTERNS):
            o_g, lse_g = _banded(qkv, bias_c[gi], None, dil=dil, blk=window // (2 * dil), group=C_HEADS_PER_GROUP,
                                 kv_heads=1, q_head0=_QC + gi * C_HEADS_PER_GROUP, k_head0=_KC + gi,
                                 v_head0=_VC + gi, emit_lse=True)
            ocs.append(o_g.reshape(b * s, -1))
            lses.append(lse_g.reshape(b * s, -1))
        x2d = _out_proj(x2d, out_a.reshape(b * s, -1), out_b.reshape(b * s, -1), ocs, lses, w_out[l])
        kv = _proj(mem2d, ln_mem[l], w_ckv[l], ("plain",) * (2 * X_HEADS), seq_len=mem_len, tn_heads=2 * X_HEADS)
        x2d = _cross(x2d, ln_cross[l], w_cq[l], kv.reshape(b, mem_len, -1), w_co[l], seq_len=s)
        x2d = _ffn(x2d, ln_ffn[l], w_ffn_in[l], w_ffn_out[l], ln_final if l == depth - 1 else None)
    return x2d.reshape(b, s, d)


def kernel(x_prompt, x_sample, mem_prompt, mem_sample, ln_mix, w_in, q_norm_a, k_norm_a, sink_b, rel_bias, w_out,
           ln_cross, ln_mem, w_cq, w_ckv, w_co, ln_ffn, w_ffn_in, w_ffn_out, ln_final):
    bias_b = _band_bias(rel_bias, 0, B_KV_HEADS, B_HEADS // B_KV_HEADS, B_WINDOW, 1)
    bias_c = [_band_bias(rel_bias, B_HEADS + gi * C_HEADS_PER_GROUP, 1, C_HEADS_PER_GROUP, w // (2 * dl), dl)
              for gi, (w, dl) in enumerate(C_PATTERNS)]
    weights = [w.astype(BF16) for w in (w_in, w_out, w_cq, w_ckv, w_co, w_ffn_in, w_ffn_out)]
    w_in_b, w_out_b, w_cq_b, w_ckv_b, w_co_b, w_ffn_in_b, w_ffn_out_b = weights

    def run(x, mem):
        return _encode(x, mem, ln_mix, w_in_b, q_norm_a, k_norm_a, sink_b, w_out_b, ln_cross, ln_mem, w_cq_b,
                       w_ckv_b, w_co_b, ln_ffn, w_ffn_in_b, w_ffn_out_b, ln_final, bias_b, bias_c)

    return (run(x_prompt, mem_prompt), run(x_sample, mem_sample))
```

```python
import functools
import math

import jax
import jax.numpy as jnp
import numpy as np
from jax import lax
from jax.experimental import pallas as pl
from jax.experimental.pallas import tpu as pltpu

HEAD_DIM = 128
SCALE = HEAD_DIM ** -0.5
A_HEADS, A_KV_HEADS = 6, 2
B_HEADS, B_KV_HEADS = 4, 2
B_WINDOW = 128
C_PATTERNS = ((128, 1), (512, 4), (2048, 16))
C_HEADS_PER_GROUP = 2
C_GROUPS = len(C_PATTERNS)
GRID_W = 64
ROPE_THETA = 10000.0
ROPE_AXIS_DIM = HEAD_DIM // 2
REL_BUCKETS = 32
REL_MAX_DIST = 1024
X_HEADS = 4
RMS_EPS = 1e-6
NEG_INF = -1e30
LOG2_E = math.log2(math.e)
BANDED_UNROLL = 4

_QA, _KA, _VA = 0, A_HEADS, A_HEADS + A_KV_HEADS
_QB = _VA + A_KV_HEADS
_KB, _VB = _QB + B_HEADS, _QB + B_HEADS + B_KV_HEADS
_QC = _VB + B_KV_HEADS
_KC = _QC + C_GROUPS * C_HEADS_PER_GROUP
_VC = _KC + C_GROUPS
PROJ_HEADS = _VC + C_GROUPS
_HEAD_TYPES = (("rope_q",) * A_HEADS + ("rope_k",) * A_KV_HEADS + ("plain",) * A_KV_HEADS
               + ("scale",) * B_HEADS + ("plain",) * (2 * B_KV_HEADS)
               + ("scale",) * (C_GROUPS * C_HEADS_PER_GROUP) + ("plain",) * (2 * C_GROUPS))
MIX_HEADS = A_HEADS + B_HEADS + C_GROUPS * C_HEADS_PER_GROUP

V7X_VMEM_LIMIT_BYTES = 56 * 1024 * 1024

BF16 = jnp.bfloat16
F32 = jnp.float32
_NT = (((1,), (1,)), ((), ()))


def _largest_tile(n, candidates):
    for c in candidates:
        if n % c == 0:
            return c
    raise ValueError(f"no tile in {candidates} divides {n}")


def _params(*sem):
    return pltpu.CompilerParams(dimension_semantics=sem, vmem_limit_bytes=V7X_VMEM_LIMIT_BYTES)


def _rms_normed(x, gain):
    return x * lax.rsqrt(jnp.mean(x * x, axis=-1, keepdims=True) + RMS_EPS) * gain


def _proj_kernel(*refs, head_types, heads_per_block):
    has_rope = any(t.startswith("rope") for t in head_types)
    if has_rope:
        x_ref, g_ref, w_ref, gains_ref, cos_ref, sin_ref, o_ref, xn_ref = refs
    else:
        x_ref, g_ref, w_ref, o_ref, xn_ref = refs
    j = pl.program_id(1)

    @pl.when(j == 0)
    def _():
        xn_ref[...] = _rms_normed(x_ref[...], g_ref[...]).astype(BF16)

    acc = jnp.dot(xn_ref[...], w_ref[...], preferred_element_type=F32)
    tm = acc.shape[0]

    def rope_head(y, gain, post):
        y = _rms_normed(y, gain)
        lane = lax.broadcasted_iota(jnp.int32, (tm, HEAD_DIM), 1)
        first_quarter = (lane % (HEAD_DIM // 2)) < (HEAD_DIM // 4)
        partner = jnp.where(first_quarter, pltpu.roll(y, 3 * HEAD_DIM // 4, 1), pltpu.roll(y, HEAD_DIM // 4, 1))
        out = y * cos_ref[...] + partner * sin_ref[...]
        return out * post if post != 1.0 else out

    def epilogue(types):
        h = 0
        while h < len(types):
            ty = types[h]
            if ty in ("rope_q", "rope_k"):
                sl = slice(h * HEAD_DIM, (h + 1) * HEAD_DIM)
                row = 0 if ty == "rope_q" else 1
                post = SCALE * LOG2_E if ty == "rope_q" else 1.0
                o_ref[:, sl] = rope_head(acc[:, sl], gains_ref[row:row + 1, :], post).astype(BF16)
                h += 1
            else:
                e = h
                while e < len(types) and types[e] == ty:
                    e += 1
                sl = slice(h * HEAD_DIM, e * HEAD_DIM)
                val = acc[:, sl] * SCALE if ty == "scale" else acc[:, sl]
                o_ref[:, sl] = val.astype(BF16)
                h = e

    n_blocks = len(head_types) // heads_per_block
    if n_blocks == 1:
        epilogue(head_types)
    else:
        for jb in range(n_blocks):
            pl.when(j == jb)(functools.partial(
                epilogue, head_types[jb * heads_per_block:(jb + 1) * heads_per_block]))


def _proj(x2d, gain, w, head_types, *, seq_len, rope=None, tn_heads):
    m, d = x2d.shape
    n = w.shape[1]
    assert n == len(head_types) * HEAD_DIM and len(head_types) % tn_heads == 0
    tm = _largest_tile(math.gcd(m, seq_len), (1024, 512, 256))
    tn = tn_heads * HEAD_DIM
    in_specs = [pl.BlockSpec((tm, d), lambda i, j: (i, 0)),
                pl.BlockSpec((1, d), lambda i, j: (0, 0)),
                pl.BlockSpec((d, tn), lambda i, j: (0, j))]
    args = [x2d, gain.reshape(1, d), w]
    if rope is not None:
        gains, cos_t, sin_t = rope
        tiles_per_seq = seq_len // tm
        in_specs += [pl.BlockSpec((2, HEAD_DIM), lambda i, j: (0, 0)),
                     pl.BlockSpec((tm, HEAD_DIM), lambda i, j: (i % tiles_per_seq, 0)),
                     pl.BlockSpec((tm, HEAD_DIM), lambda i, j: (i % tiles_per_seq, 0))]
        args += [gains, cos_t, sin_t]
    return pl.pallas_call(
        functools.partial(_proj_kernel, head_types=tuple(head_types), heads_per_block=tn_heads),
        grid=(m // tm, n // tn),
        in_specs=in_specs,
        out_specs=pl.BlockSpec((tm, tn), lambda i, j: (i, j)),
        out_shape=jax.ShapeDtypeStruct((m, n), BF16),
        scratch_shapes=[pltpu.VMEM((tm, d), BF16)],
        compiler_params=_params("parallel", "arbitrary"),
        name="proj",
    )(*args)


def _attn_a_kernel(q_ref, k_ref, v_ref, o_ref, q_scr, s0_scr, s1_scr, m_scr, l_scr, acc_scr, *, group, tk):
    tq = q_ref.shape[0]
    n_chunks = k_ref.shape[0] // tk
    n_slabs = tk // HEAD_DIM
    for h in range(group):
        q_scr[h * tq:(h + 1) * tq, :] = q_ref[:, h * HEAD_DIM:(h + 1) * HEAD_DIM]
    m_scr[...] = jnp.full(m_scr.shape, -jnp.inf, F32)
    l_scr[...] = jnp.zeros(l_scr.shape, F32)
    acc_scr[...] = jnp.zeros(acc_scr.shape, F32)

    def scores(c, s_scr):
        off = pl.multiple_of(c * tk, tk)
        s_scr[...] = lax.dot_general(q_scr[...], k_ref[pl.ds(off, tk), :], _NT, preferred_element_type=F32)

    def accumulate(c, s_scr):
        off = pl.multiple_of(c * tk, tk)
        slabs = [s_scr[:, j * HEAD_DIM:(j + 1) * HEAD_DIM] for j in range(n_slabs)]
        m_prev = m_scr[...]
        m_new = jnp.maximum(m_prev, jnp.max(functools.reduce(jnp.maximum, slabs), axis=-1, keepdims=True))
        a = jnp.exp2(m_prev - m_new)
        ps = [jnp.exp2(s - m_new) for s in slabs]
        l_scr[...] = a * l_scr[...] + functools.reduce(jnp.add, ps)
        p = jnp.concatenate([p.astype(BF16) for p in ps], axis=1)
        acc_scr[...] = a * acc_scr[...] + jnp.dot(p, v_ref[pl.ds(off, tk), :], preferred_element_type=F32)
        m_scr[...] = m_new

    scores(0, s0_scr)

    def pair(cc, carry):
        c = 2 * cc
        scores(c + 1, s1_scr)
        accumulate(c, s0_scr)
        scores(c + 2, s0_scr)
        accumulate(c + 1, s1_scr)
        return carry

    lax.fori_loop(0, n_chunks // 2 - 1, pair, 0)
    scores(n_chunks - 1, s1_scr)
    accumulate(n_chunks - 2, s0_scr)
    accumulate(n_chunks - 1, s1_scr)
    out = acc_scr[...] / jnp.sum(l_scr[...], axis=-1, keepdims=True)
    for h in range(group):
        o_ref[:, h * HEAD_DIM:(h + 1) * HEAD_DIM] = out[h * tq:(h + 1) * tq, :].astype(BF16)


def _attn_a(qkv, *, tq=256, tk=512):
    b, s, _ = qkv.shape
    group = A_HEADS // A_KV_HEADS
    gw = group * HEAD_DIM
    assert s % tq == 0 and s % (2 * tk) == 0 and _QA == 0
    return pl.pallas_call(
        functools.partial(_attn_a_kernel, group=group, tk=tk),
        grid=(b, A_KV_HEADS, s // tq),
        in_specs=[pl.BlockSpec((None, tq, gw), lambda bi, kv, i: (bi, i, kv)),
                  pl.BlockSpec((None, s, HEAD_DIM), lambda bi, kv, i: (bi, 0, _KA + kv)),
                  pl.BlockSpec((None, s, HEAD_DIM), lambda bi, kv, i: (bi, 0, _VA + kv))],
        out_specs=pl.BlockSpec((None, tq, gw), lambda bi, kv, i: (bi, i, kv)),
        out_shape=jax.ShapeDtypeStruct((b, s, A_HEADS * HEAD_DIM), BF16),
        scratch_shapes=[pltpu.VMEM((group * tq, HEAD_DIM), BF16),
                        pltpu.VMEM((group * tq, tk), F32),
                        pltpu.VMEM((group * tq, tk), F32),
                        pltpu.VMEM((group * tq, HEAD_DIM), F32),
                        pltpu.VMEM((group * tq, HEAD_DIM), F32),
                        pltpu.VMEM((group * tq, HEAD_DIM), F32)],
        compiler_params=_params("parallel", "parallel", "arbitrary"),
        name="attn_a",
    )(qkv, qkv, qkv)


def _banded_kernel(*refs, blk, group, has_sink, emit_lse):
    q_ref, k_ref, v_ref, bias_ref = refs[:4]
    refs = refs[4:]
    sink_ref = None
    if has_sink:
        sink_ref, refs = refs[0], refs[1:]
    o_ref = refs[0]
    lse_ref = refs[1] if emit_lse else None
    tl = q_ref.shape[0]
    n_blocks = k_ref.shape[0] // blk
    blocks_per_tile = tl // blk
    t = pl.program_id(3)

    def block(i, carry):
        n = t * blocks_per_tile + i
        row0 = pl.multiple_of(i * blk, blk)
        start = pl.multiple_of(jnp.clip(n - 1, 0, n_blocks - 3) * blk, blk)
        variant = jnp.where(n == 0, 1, jnp.where(n == n_blocks - 1, 2, 0))
        q = jnp.concatenate([q_ref[pl.ds(row0, blk), h * HEAD_DIM:(h + 1) * HEAD_DIM] for h in range(group)], axis=0)
        kw = k_ref[pl.ds(start, 3 * blk), :]
        vw = v_ref[pl.ds(start, 3 * blk), :]
        s = lax.dot_general(q, kw, _NT, preferred_element_type=F32) + bias_ref[variant]
        m = jnp.max(s, axis=-1, keepdims=True)
        if has_sink:
            sk = sink_ref[...]
            m = jnp.maximum(m, sk)
        e = jnp.exp(s - m)
        den = jnp.sum(e, axis=-1, keepdims=True)
        if has_sink:
            den = den + jnp.exp(sk - m)
        o = jnp.dot(e.astype(BF16), vw, preferred_element_type=F32) / den
        for h in range(group):
            o_ref[pl.ds(row0, blk), h * HEAD_DIM:(h + 1) * HEAD_DIM] = o[h * blk:(h + 1) * blk, :].astype(BF16)
        if emit_lse:
            lse = m + jnp.log(den)
            for h in range(group):
                lse_ref[pl.ds(row0, blk), h * HEAD_DIM:(h + 1) * HEAD_DIM] = jnp.broadcast_to(
                    lse[h * blk:(h + 1) * blk, :], (blk, HEAD_DIM))
        return carry

    lax.fori_loop(0, blocks_per_tile, block, 0, unroll=BANDED_UNROLL)


def _banded(qkv, bias, sink_col, *, dil, blk, group, kv_heads, q_head0, k_head0, v_head0, emit_lse):
    b, s, width = qkv.shape
    heads_per_row = width // HEAD_DIM
    length = s // dil
    assert s % dil == 0 and length % blk == 0 and length // blk >= 3
    tl = _largest_tile(length, (2048, 1024, 512, 256))
    gw = group * HEAD_DIM
    assert q_head0 % group == 0 and heads_per_row % group == 0
    qkv_r = qkv.reshape(b, length, dil * width)
    in_specs = [
        pl.BlockSpec((None, tl, gw), lambda bi, r, kv, t: (bi, t, (r * heads_per_row + q_head0) // group + kv)),
        pl.BlockSpec((None, length, HEAD_DIM), lambda bi, r, kv, t: (bi, 0, r * heads_per_row + k_head0 + kv)),
        pl.BlockSpec((None, length, HEAD_DIM), lambda bi, r, kv, t: (bi, 0, r * heads_per_row + v_head0 + kv)),
        pl.BlockSpec((None, 3, group * blk, 3 * blk), lambda bi, r, kv, t: (kv, 0, 0, 0)),
    ]
    args = [qkv_r, qkv_r, qkv_r, bias]
    if sink_col is not None:
        in_specs.append(pl.BlockSpec((None, group * blk, 1), lambda bi, r, kv, t: (kv, 0, 0)))
        args.append(sink_col)
    out_w = kv_heads * gw
    o_spec = pl.BlockSpec((None, tl, gw), lambda bi, r, kv, t: (bi, t, r * kv_heads + kv))
    out_shape = [jax.ShapeDtypeStruct((b, length, dil * out_w), BF16)]
    out_specs = [o_spec]
    if emit_lse:
        out_shape.append(jax.ShapeDtypeStruct((b, length, dil * out_w), F32))
        out_specs.append(o_spec)
    outs = pl.pallas_call(
        functools.partial(_banded_kernel, blk=blk, group=group, has_sink=sink_col is not None, emit_lse=emit_lse),
        grid=(b, dil, kv_heads, length // tl),
        in_specs=in_specs,
        out_specs=out_specs,
        out_shape=out_shape,
        compiler_params=_params("parallel", "parallel", "parallel", "arbitrary"),
        name=f"banded_d{dil}",
    )(*args)
    return [o.reshape(b, s, out_w) for o in outs]


def _out_proj_kernel(oa_ref, ob_ref, oc0_ref, oc1_ref, oc2_ref, l0_ref, l1_ref, l2_ref, x_ref, w_ref,
                     out_ref, lhs_ref):
    j = pl.program_id(1)

    @pl.when(j == 0)
    def _():
        a_w = oa_ref.shape[1]
        b_w = ob_ref.shape[1]
        c_w = oc0_ref.shape[1]
        lhs_ref[:, 0:a_w] = oa_ref[...]
        lhs_ref[:, a_w:a_w + b_w] = ob_ref[...]
        ls = [l0_ref[...], l1_ref[...], l2_ref[...]]
        mx = jnp.maximum(jnp.maximum(ls[0], ls[1]), ls[2])
        es = [jnp.exp(l - mx) for l in ls]
        tot = es[0] + es[1] + es[2]
        for gi, oc_ref in enumerate((oc0_ref, oc1_ref, oc2_ref)):
            c0 = a_w + b_w + gi * c_w
            lhs_ref[:, c0:c0 + c_w] = (oc_ref[...].astype(F32) * (es[gi] / tot)).astype(BF16)

    out_ref[...] = x_ref[...] + jnp.dot(lhs_ref[...], w_ref[...], preferred_element_type=F32)


def _out_proj(x2d, oa, ob, ocs, lses, w):
    m, d = x2d.shape
    k = w.shape[0]
    tm = _largest_tile(m, (1024, 512, 256))
    tn = _largest_tile(d, (512, 256, 128))
    row = lambda i, j: (i, 0)
    lhs = [oa, ob, *ocs, *lses]
    assert sum(a.shape[1] for a in (oa, ob, *ocs)) == k
    return pl.pallas_call(
        _out_proj_kernel,
        grid=(m // tm, d // tn),
        in_specs=[pl.BlockSpec((tm, a.shape[1]), row) for a in lhs]
        + [pl.BlockSpec((tm, tn), lambda i, j: (i, j)), pl.BlockSpec((k, tn), lambda i, j: (0, j))],
        out_specs=pl.BlockSpec((tm, tn), lambda i, j: (i, j)),
        out_shape=jax.ShapeDtypeStruct((m, d), F32),
        scratch_shapes=[pltpu.VMEM((tm, k), BF16)],
        compiler_params=_params("parallel", "arbitrary"),
        name="out_proj",
    )(*lhs, x2d, w)


def _cross_kernel(x_ref, g_ref, wq_ref, kv_ref, wo_ref, out_ref):
    x = x_ref[...]
    xn = _rms_normed(x, g_ref[...]).astype(BF16)
    q = (jnp.dot(xn, wq_ref[...], preferred_element_type=F32) * SCALE).astype(BF16)
    xw = X_HEADS * HEAD_DIM
    heads = []
    for h in range(X_HEADS):
        sl = slice(h * HEAD_DIM, (h + 1) * HEAD_DIM)
        k = kv_ref[:, sl]
        v = kv_ref[:, xw + h * HEAD_DIM:xw + (h + 1) * HEAD_DIM]
        s = lax.dot_general(q[:, sl], k, _NT, preferred_element_type=F32)
        e = jnp.exp(s - jnp.max(s, axis=-1, keepdims=True))
        den = jnp.sum(e, axis=-1, keepdims=True)
        heads.append((jnp.dot(e.astype(BF16), v, preferred_element_type=F32) / den).astype(BF16))
    o = jnp.concatenate(heads, axis=1)
    out_ref[...] = x + jnp.dot(o, wo_ref[...], preferred_element_type=F32)


def _cross(x2d, gain, wq, kv, wo, *, seq_len):
    m, d = x2d.shape
    tm = _largest_tile(math.gcd(m, seq_len), (512, 256))
    tiles_per_seq = seq_len // tm
    mem_len, kvw = kv.shape[1], kv.shape[2]
    return pl.pallas_call(
        _cross_kernel,
        grid=(m // tm,),
        in_specs=[pl.BlockSpec((tm, d), lambda i: (i, 0)),
                  pl.BlockSpec((1, d), lambda i: (0, 0)),
                  pl.BlockSpec(wq.shape, lambda i: (0, 0)),
                  pl.BlockSpec((None, mem_len, kvw), lambda i: (i // tiles_per_seq, 0, 0)),
                  pl.BlockSpec(wo.shape, lambda i: (0, 0))],
        out_specs=pl.BlockSpec((tm, d), lambda i: (i, 0)),
        out_shape=jax.ShapeDtypeStruct((m, d), F32),
        compiler_params=_params("parallel"),
        name="cross",
    )(x2d, gain.reshape(1, d), wq, kv, wo)


def _ffn_kernel(*refs, final_norm):
    if final_norm:
        x_ref, g_ref, wg_ref, wu_ref, wo_ref, gf_ref, out_ref, xn_ref = refs
    else:
        x_ref, g_ref, wg_ref, wu_ref, wo_ref, out_ref, xn_ref = refs
    c = pl.program_id(1)

    @pl.when(c == 0)
    def _():
        x = x_ref[...]
        xn_ref[...] = _rms_normed(x, g_ref[...]).astype(BF16)
        out_ref[...] = x

    h = xn_ref[...]
    gate = jnp.dot(h, wg_ref[...], preferred_element_type=F32)
    up = jnp.dot(h, wu_ref[...], preferred_element_type=F32)
    act = (gate * jax.nn.sigmoid(gate) * up).astype(BF16)
    out_ref[...] += jnp.dot(act, wo_ref[...], preferred_element_type=F32)

    if final_norm:
        @pl.when(c == pl.num_programs(1) - 1)
        def _():
            out_ref[...] = _rms_normed(out_ref[...], gf_ref[...])


def _ffn(x2d, gain, w_in, w_out, final_gain=None):
    m, d = x2d.shape
    d_ff = w_out.shape[0]
    assert w_in.shape == (d, 2 * d_ff)
    tm = _largest_tile(m, (512, 256))
    tf = _largest_tile(d_ff, (512, 256, 128))
    n_chunks = d_ff // tf
    in_specs = [pl.BlockSpec((tm, d), lambda i, c: (i, 0)),
                pl.BlockSpec((1, d), lambda i, c: (0, 0)),
                pl.BlockSpec((d, tf), lambda i, c: (0, c)),
                pl.BlockSpec((d, tf), lambda i, c: (0, c + n_chunks)),
                pl.BlockSpec((tf, d), lambda i, c: (c, 0))]
    args = [x2d, gain.reshape(1, d), w_in, w_in, w_out]
    if final_gain is not None:
        in_specs.append(pl.BlockSpec((1, d), lambda i, c: (0, 0)))
        args.append(final_gain.reshape(1, d))
    return pl.pallas_call(
        functools.partial(_ffn_kernel, final_norm=final_gain is not None),
        grid=(m // tm, n_chunks),
        in_specs=in_specs,
        out_specs=pl.BlockSpec((tm, d), lambda i, c: (i, 0)),
        out_shape=jax.ShapeDtypeStruct((m, d), F32),
        scratch_shapes=[pltpu.VMEM((tm, d), BF16)],
        compiler_params=_params("parallel", "arbitrary"),
        name="ffn",
    )(*args)


def _rope_tables(seq_len):
    rows = seq_len // GRID_W
    row = jnp.repeat(jnp.arange(rows), GRID_W).astype(F32)
    col = jnp.tile(jnp.arange(GRID_W), rows).astype(F32)
    inv = ROPE_THETA ** (-jnp.arange(0, ROPE_AXIS_DIM, 2, dtype=F32) / ROPE_AXIS_DIM)
    ang_r = row[:, None] * inv
    ang_c = col[:, None] * inv
    cos_t = jnp.concatenate([jnp.cos(ang_r)] * 2 + [jnp.cos(ang_c)] * 2, axis=-1)
    sin_t = jnp.concatenate([-jnp.sin(ang_r), jnp.sin(ang_r), -jnp.sin(ang_c), jnp.sin(ang_c)], axis=-1)
    return cos_t, sin_t


def _t5_bucket(rel):
    nb = REL_BUCKETS // 2
    max_exact = nb // 2
    ret = jnp.where(rel > 0, nb, 0)
    n = jnp.abs(rel)
    large = max_exact + (jnp.log(jnp.maximum(n, 1).astype(F32) / max_exact)
                         / math.log(REL_MAX_DIST / max_exact) * (nb - max_exact)).astype(jnp.int32)
    large = jnp.minimum(large, nb - 1)
    return ret + jnp.where(n < max_exact, n, large)


def _band_bias(rel_bias, head0, kv_heads, group, blk, dil):
    off = (jnp.arange(3 * blk) - blk)[None, :] - jnp.arange(blk)[:, None]
    table = rel_bias[:, head0:head0 + kv_heads * group].astype(F32)
    bias = jnp.moveaxis(table[_t5_bucket(off * dil)], -1, 0)
    bias = jnp.where((jnp.abs(off) <= blk)[None], bias, NEG_INF)
    masked = jnp.full((kv_heads * group, blk, blk), NEG_INF, F32)
    first = jnp.concatenate([bias[:, :, blk:], masked], axis=-1)
    last = jnp.concatenate([masked, bias[:, :, :2 * blk]], axis=-1)
    variants = jnp.stack([bias, first, last], axis=1)
    variants = variants.reshape(kv_heads, group, 3, blk, 3 * blk)
    return jnp.moveaxis(variants, 1, 2).reshape(kv_heads, 3, group * blk, 3 * blk)


def _encode(x, mem, ln_mix, w_in, q_norm_a, k_norm_a, sink_b, w_out, ln_cross, ln_mem, w_cq, w_ckv, w_co,
            ln_ffn, w_ffn_in, w_ffn_out, ln_final, bias_b, bias_c):
    b, s, d = x.shape
    mem_len = mem.shape[1]
    depth = w_in.shape[0]
    cos_t, sin_t = _rope_tables(s)
    x2d = x.reshape(b * s, d)
    mem2d = mem.reshape(b * mem_len, d)
    b_group = B_HEADS // B_KV_HEADS
    for l in range(depth):
        gains = jnp.stack([q_norm_a[l], k_norm_a[l]]).astype(F32)
        qkv = _proj(x2d, ln_mix[l], w_in[l], _HEAD_TYPES, seq_len=s, rope=(gains, cos_t, sin_t), tn_heads=10)
        qkv = qkv.reshape(b, s, PROJ_HEADS * HEAD_DIM)
        out_a = _attn_a(qkv)
        sink_col = jnp.repeat(sink_b[l].astype(F32).reshape(B_KV_HEADS, b_group), B_WINDOW, axis=1)[..., None]
        (out_b,) = _banded(qkv, bias_b, sink_col, dil=1, blk=B_WINDOW, group=b_group, kv_heads=B_KV_HEADS,
                           q_head0=_QB, k_head0=_KB, v_head0=_VB, emit_lse=False)
        ocs, lses = [], []
        for gi, (window, dil) in enumerate(C_PATTERNS):
            o_g, lse_g = _banded(qkv, bias_c[gi], None, dil=dil, blk=window // (2 * dil), group=C_HEADS_PER_GROUP,
                                 kv_heads=1, q_head0=_QC + gi * C_HEADS_PER_GROUP, k_head0=_KC + gi,
                                 v_head0=_VC + gi, emit_lse=True)
            ocs.append(o_g.reshape(b * s, -1))
            lses.append(lse_g.reshape(b * s, -1))
        x2d = _out_proj(x2d, out_a.reshape(b * s, -1), out_b.reshape(b * s, -1), ocs, lses, w_out[l])
        kv = _proj(mem2d, ln_mem[l], w_ckv[l], ("plain",) * (2 * X_HEADS), seq_len=mem_len, tn_heads=2 * X_HEADS)
        x2d = _cross(x2d, ln_cross[l], w_cq[l], kv.reshape(b, mem_len, -1), w_co[l], seq_len=s)
        x2d = _ffn(x2d, ln_ffn[l], w_ffn_in[l], w_ffn_out[l], ln_final if l == depth - 1 else None)
    return x2d.reshape(b, s, d)


def kernel(x_prompt, x_sample, mem_prompt, mem_sample, ln_mix, w_in, q_norm_a, k_norm_a, sink_b, rel_bias, w_out,
           ln_cross, ln_mem, w_cq, w_ckv, w_co, ln_ffn, w_ffn_in, w_ffn_out, ln_final):
    bias_b = _band_bias(rel_bias, 0, B_KV_HEADS, B_HEADS // B_KV_HEADS, B_WINDOW, 1)
    bias_c = [_band_bias(rel_bias, B_HEADS + gi * C_HEADS_PER_GROUP, 1, C_HEADS_PER_GROUP, w // (2 * dl), dl)
              for gi, (w, dl) in enumerate(C_PATTERNS)]
    weights = [w.astype(BF16) for w in (w_in, w_out, w_cq, w_ckv, w_co, w_ffn_in, w_ffn_out)]
    w_in_b, w_out_b, w_cq_b, w_ckv_b, w_co_b, w_ffn_in_b, w_ffn_out_b = weights

    def run(x, mem):
        return _encode(x, mem, ln_mix, w_in_b, q_norm_a, k_norm_a, sink_b, w_out_b, ln_cross, ln_mem, w_cq_b,
                       w_ckv_b, w_co_b, ln_ffn, w_ffn_in_b, w_ffn_out_b, ln_final, bias_b, bias_c)

    return (run(x_prompt, mem_prompt), run(x_sample, mem_sample))
```

```python
import functools
import math

import jax
import jax.numpy as jnp
import numpy as np
from jax import lax
from jax.experimental import pallas as pl
from jax.experimental.pallas import tpu as pltpu

HEAD_DIM = 128
SCALE = HEAD_DIM ** -0.5
A_HEADS, A_KV_HEADS = 6, 2
B_HEADS, B_KV_HEADS = 4, 2
B_WINDOW = 128
C_PATTERNS = ((128, 1), (512, 4), (2048, 16))
C_HEADS_PER_GROUP = 2
C_GROUPS = len(C_PATTERNS)
GRID_W = 64
ROPE_THETA = 10000.0
ROPE_AXIS_DIM = HEAD_DIM // 2
REL_BUCKETS = 32
REL_MAX_DIST = 1024
X_HEADS = 4
RMS_EPS = 1e-6
NEG_INF = -1e30
LOG2_E = math.log2(math.e)
BANDED_UNROLL = 4

_QA, _KA, _VA = 0, A_HEADS, A_HEADS + A_KV_HEADS
_QB = _VA + A_KV_HEADS
_KB, _VB = _QB + B_HEADS, _QB + B_HEADS + B_KV_HEADS
_QC = _VB + B_KV_HEADS
_KC = _QC + C_GROUPS * C_HEADS_PER_GROUP
_VC = _KC + C_GROUPS
PROJ_HEADS = _VC + C_GROUPS
_HEAD_TYPES = (("rope_q",) * A_HEADS + ("rope_k",) * A_KV_HEADS + ("plain",) * A_KV_HEADS
               + ("scale",) * B_HEADS + ("plain",) * (2 * B_KV_HEADS)
               + ("scale",) * (C_GROUPS * C_HEADS_PER_GROUP) + ("plain",) * (2 * C_GROUPS))
MIX_HEADS = A_HEADS + B_HEADS + C_GROUPS * C_HEADS_PER_GROUP

V7X_VMEM_LIMIT_BYTES = 56 * 1024 * 1024

BF16 = jnp.bfloat16
F32 = jnp.float32
_NT = (((1,), (1,)), ((), ()))


def _largest_tile(n, candidates):
    for c in candidates:
        if n % c == 0:
            return c
    raise ValueError(f"no tile in {candidates} divides {n}")


def _params(*sem):
    return pltpu.CompilerParams(dimension_semantics=sem, vmem_limit_bytes=V7X_VMEM_LIMIT_BYTES)


def _rms_normed(x, gain):
    return x * lax.rsqrt(jnp.mean(x * x, axis=-1, keepdims=True) + RMS_EPS) * gain


def _proj_kernel(*refs, head_types, heads_per_block):
    has_rope = any(t.startswith("rope") for t in head_types)
    if has_rope:
        x_ref, g_ref, w_ref, gains_ref, cos_ref, sin_ref, o_ref, xn_ref = refs
    else:
        x_ref, g_ref, w_ref, o_ref, xn_ref = refs
    j = pl.program_id(1)

    @pl.when(j == 0)
    def _():
        xn_ref[...] = _rms_normed(x_ref[...], g_ref[...]).astype(BF16)

    acc = jnp.dot(xn_ref[...], w_ref[...], preferred_element_type=F32)
    tm = acc.shape[0]

    def rope_head(y, gain, post):
        y = _rms_normed(y, gain)
        lane = lax.broadcasted_iota(jnp.int32, (tm, HEAD_DIM), 1)
        first_quarter = (lane % (HEAD_DIM // 2)) < (HEAD_DIM // 4)
        partner = jnp.where(first_quarter, pltpu.roll(y, 3 * HEAD_DIM // 4, 1), pltpu.roll(y, HEAD_DIM // 4, 1))
        out = y * cos_ref[...] + partner * sin_ref[...]
        return out * post if post != 1.0 else out

    def epilogue(types):
        h = 0
        while h < len(types):
            ty = types[h]
            if ty in ("rope_q", "rope_k"):
                sl = slice(h * HEAD_DIM, (h + 1) * HEAD_DIM)
                row = 0 if ty == "rope_q" else 1
                post = SCALE * LOG2_E if ty == "rope_q" else 1.0
                o_ref[:, sl] = rope_head(acc[:, sl], gains_ref[row:row + 1, :], post).astype(BF16)
                h += 1
            else:
                e = h
                while e < len(types) and types[e] == ty:
                    e += 1
                sl = slice(h * HEAD_DIM, e * HEAD_DIM)
                val = acc[:, sl] * SCALE if ty == "scale" else acc[:, sl]
                o_ref[:, sl] = val.astype(BF16)
                h = e

    n_blocks = len(head_types) // heads_per_block
    if n_blocks == 1:
        epilogue(head_types)
    else:
        for jb in range(n_blocks):
            pl.when(j == jb)(functools.partial(
                epilogue, head_types[jb * heads_per_block:(jb + 1) * heads_per_block]))


def _proj(x2d, gain, w, head_types, *, seq_len, rope=None, tn_heads):
    m, d = x2d.shape
    n = w.shape[1]
    assert n == len(head_types) * HEAD_DIM and len(head_types) % tn_heads == 0
    tm = _largest_tile(math.gcd(m, seq_len), (1024, 512, 256))
    tn = tn_heads * HEAD_DIM
    in_specs = [pl.BlockSpec((tm, d), lambda i, j: (i, 0)),
                pl.BlockSpec((1, d), lambda i, j: (0, 0)),
                pl.BlockSpec((d, tn), lambda i, j: (0, j))]
    args = [x2d, gain.reshape(1, d), w]
    if rope is not None:
        gains, cos_t, sin_t = rope
        tiles_per_seq = seq_len // tm
        in_specs += [pl.BlockSpec((2, HEAD_DIM), lambda i, j: (0, 0)),
                     pl.BlockSpec((tm, HEAD_DIM), lambda i, j: (i % tiles_per_seq, 0)),
                     pl.BlockSpec((tm, HEAD_DIM), lambda i, j: (i % tiles_per_seq, 0))]
        args += [gains, cos_t, sin_t]
    return pl.pallas_call(
        functools.partial(_proj_kernel, head_types=tuple(head_types), heads_per_block=tn_heads),
        grid=(m // tm, n // tn),
        in_specs=in_specs,
        out_specs=pl.BlockSpec((tm, tn), lambda i, j: (i, j)),
        out_shape=jax.ShapeDtypeStruct((m, n), BF16),
        scratch_shapes=[pltpu.VMEM((tm, d), BF16)],
        compiler_params=_params("parallel", "arbitrary"),
        name="proj",
    )(*args)


def _attn_a_kernel(q_ref, k_ref, v_ref, o_ref, q_scr, s0_scr, s1_scr, m_scr, l_scr, acc_scr, *, group, tk):
    tq = q_ref.shape[0]
    n_chunks = k_ref.shape[0] // tk
    n_slabs = tk // HEAD_DIM
    for h in range(group):
        q_scr[h * tq:(h + 1) * tq, :] = q_ref[:, h * HEAD_DIM:(h + 1) * HEAD_DIM]
    m_scr[...] = jnp.full(m_scr.shape, -jnp.inf, F32)
    l_scr[...] = jnp.zeros(l_scr.shape, F32)
    acc_scr[...] = jnp.zeros(acc_scr.shape, F32)

    def scores(c, s_scr):
        off = pl.multiple_of(c * tk, tk)
        s_scr[...] = lax.dot_general(q_scr[...], k_ref[pl.ds(off, tk), :], _NT, preferred_element_type=F32)

    def accumulate(c, s_scr):
        off = pl.multiple_of(c * tk, tk)
        slabs = [s_scr[:, j * HEAD_DIM:(j + 1) * HEAD_DIM] for j in range(n_slabs)]
        m_prev = m_scr[...]
        m_new = jnp.maximum(m_prev, jnp.max(functools.reduce(jnp.maximum, slabs), axis=-1, keepdims=True))
        a = jnp.exp2(m_prev - m_new)
        ps = [jnp.exp2(s - m_new) for s in slabs]
        l_scr[...] = a * l_scr[...] + functools.reduce(jnp.add, ps)
        p = jnp.concatenate([p.astype(BF16) for p in ps], axis=1)
        acc_scr[...] = a * acc_scr[...] + jnp.dot(p, v_ref[pl.ds(off, tk), :], preferred_element_type=F32)
        m_scr[...] = m_new

    scores(0, s0_scr)

    def pair(cc, carry):
        c = 2 * cc
        scores(c + 1, s1_scr)
        accumulate(c, s0_scr)
        scores(jnp.minimum(c + 2, n_chunks - 1), s0_scr)
        accumulate(c + 1, s1_scr)
        return carry

    lax.fori_loop(0, n_chunks // 2, pair, 0)
    out =acc_scr[...] / jnp.sum(l_scr[...], axis=-1, keepdims=True)
    for h in range(group):
        o_ref[:, h * HEAD_DIM:(h + 1) * HEAD_DIM] = out[h * tq:(h + 1) * tq, :].astype(BF16)


def _attn_a(qkv, *, tq=512, tk=512):
    b, s, _ = qkv.shape
    group = A_HEADS // A_KV_HEADS
    gw = group * HEAD_DIM
    assert s % tq == 0 and s % (2 * tk) == 0 and _QA == 0
    return pl.pallas_call(
        functools.partial(_attn_a_kernel, group=group, tk=tk),
        grid=(b, A_KV_HEADS, s // tq),
        in_specs=[pl.BlockSpec((None, tq, gw), lambda bi, kv, i: (bi, i, kv)),
                  pl.BlockSpec((None, s, HEAD_DIM), lambda bi, kv, i: (bi, 0, _KA + kv)),
                  pl.BlockSpec((None, s, HEAD_DIM), lambda bi, kv, i: (bi, 0, _VA + kv))],
        out_specs=pl.BlockSpec((None, tq, gw), lambda bi, kv, i: (bi, i, kv)),
        out_shape=jax.ShapeDtypeStruct((b, s, A_HEADS * HEAD_DIM), BF16),
        scratch_shapes=[pltpu.VMEM((group * tq, HEAD_DIM), BF16),
                        pltpu.VMEM((group * tq, tk), F32),
                        pltpu.VMEM((group * tq, tk), F32),
                        pltpu.VMEM((group * tq, HEAD_DIM), F32),
                        pltpu.VMEM((group * tq, HEAD_DIM), F32),
                        pltpu.VMEM((group * tq, HEAD_DIM), F32)],
        compiler_params=_params("parallel", "parallel", "arbitrary"),
        name="attn_a",
    )(qkv, qkv, qkv)


def _banded_kernel(*refs, blk, group, has_sink, emit_lse):
    q_ref, k_ref, v_ref, bias_ref = refs[:4]
    refs = refs[4:]
    sink_ref = None
    if has_sink:
        sink_ref, refs = refs[0], refs[1:]
    o_ref = refs[0]
    lse_ref = refs[1] if emit_lse else None
    tl = q_ref.shape[0]
    n_blocks = k_ref.shape[0] // blk
    blocks_per_tile = tl // blk
    t = pl.program_id(3)

    def block(i, carry):
        n = t * blocks_per_tile + i
        row0 = pl.multiple_of(i * blk, blk)
        start = pl.multiple_of(jnp.clip(n - 1, 0, n_blocks - 3) * blk, blk)
        variant = jnp.where(n == 0, 1, jnp.where(n == n_blocks - 1, 2, 0))
        q = jnp.concatenate([q_ref[pl.ds(row0, blk), h * HEAD_DIM:(h + 1) * HEAD_DIM] for h in range(group)], axis=0)
        kw = k_ref[pl.ds(start, 3 * blk), :]
        vw = v_ref[pl.ds(start, 3 * blk), :]
        s = lax.dot_general(q, kw, _NT, preferred_element_type=F32) + bias_ref[variant]
        m = jnp.max(s, axis=-1, keepdims=True)
        if has_sink:
            sk = sink_ref[...]
            m = jnp.maximum(m, sk)
        e = jnp.exp(s - m)
        den = jnp.sum(e, axis=-1, keepdims=True)
        if has_sink:
            den = den + jnp.exp(sk - m)
        o = jnp.dot(e.astype(BF16), vw, preferred_element_type=F32) / den
        for h in range(group):
            o_ref[pl.ds(row0, blk), h * HEAD_DIM:(h + 1) * HEAD_DIM] = o[h * blk:(h + 1) * blk, :].astype(BF16)
        if emit_lse:
            lse = m + jnp.log(den)
            for h in range(group):
                lse_ref[pl.ds(row0, blk), h * HEAD_DIM:(h + 1) * HEAD_DIM] = jnp.broadcast_to(
                    lse[h * blk:(h + 1) * blk, :], (blk, HEAD_DIM))
        return carry

    lax.fori_loop(0, blocks_per_tile, block, 0, unroll=BANDED_UNROLL)


def _banded(qkv, bias, sink_col, *, dil, blk, group, kv_heads, q_head0, k_head0, v_head0, emit_lse):
    b, s, width = qkv.shape
    heads_per_row = width // HEAD_DIM
    length = s // dil
    assert s % dil == 0 and length % blk == 0 and length // blk >= 3
    tl = _largest_tile(length, (2048, 1024, 512, 256))
    gw = group * HEAD_DIM
    assert q_head0 % group == 0 and heads_per_row % group == 0
    qkv_r = qkv.reshape(b, length, dil * width)
    in_specs = [
        pl.BlockSpec((None, tl, gw), lambda bi, r, kv, t: (bi, t, (r * heads_per_row + q_head0) // group + kv)),
        pl.BlockSpec((None, length, HEAD_DIM), lambda bi, r, kv, t: (bi, 0, r * heads_per_row + k_head0 + kv)),
        pl.BlockSpec((None, length, HEAD_DIM), lambda bi, r, kv, t: (bi, 0, r * heads_per_row + v_head0 + kv)),
        pl.BlockSpec((None, 3, group * blk, 3 * blk), lambda bi, r, kv, t: (kv, 0, 0, 0)),
    ]
    args = [qkv_r, qkv_r, qkv_r, bias]
    if sink_col is not None:
        in_specs.append(pl.BlockSpec((None, group * blk, 1), lambda bi, r, kv, t: (kv, 0, 0)))
        args.append(sink_col)
    out_w = kv_heads * gw
    o_spec = pl.BlockSpec((None, tl, gw), lambda bi, r, kv, t: (bi, t, r * kv_heads + kv))
    out_shape = [jax.ShapeDtypeStruct((b, length, dil * out_w), BF16)]
    out_specs = [o_spec]
    if emit_lse:
        out_shape.append(jax.ShapeDtypeStruct((b, length, dil * out_w), F32))
        out_specs.append(o_spec)
    outs = pl.pallas_call(
        functools.partial(_banded_kernel, blk=blk, group=group, has_sink=sink_col is not None, emit_lse=emit_lse),
        grid=(b, dil, kv_heads, length // tl),
        in_specs=in_specs,
        out_specs=out_specs,
        out_shape=out_shape,
        compiler_params=_params("parallel", "parallel", "parallel", "arbitrary"),
        name=f"banded_d{dil}",
    )(*args)
    return [o.reshape(b, s, out_w) for o in outs]


def _out_proj_kernel(oa_ref, ob_ref, oc0_ref, oc1_ref, oc2_ref, l0_ref, l1_ref, l2_ref, x_ref, w_ref,
                     out_ref, lhs_ref):
    j = pl.program_id(1)

    @pl.when(j == 0)
    def _():
        a_w = oa_ref.shape[1]
        b_w = ob_ref.shape[1]
        c_w = oc0_ref.shape[1]
        lhs_ref[:, 0:a_w] = oa_ref[...]
        lhs_ref[:, a_w:a_w + b_w] = ob_ref[...]
        ls = [l0_ref[...], l1_ref[...], l2_ref[...]]
        mx = jnp.maximum(jnp.maximum(ls[0], ls[1]), ls[2])
        es = [jnp.exp(l - mx) for l in ls]
        tot = es[0] + es[1] + es[2]
        for gi, oc_ref in enumerate((oc0_ref, oc1_ref, oc2_ref)):
            c0 = a_w + b_w + gi * c_w
            lhs_ref[:, c0:c0 + c_w] = (oc_ref[...].astype(F32) * (es[gi] / tot)).astype(BF16)

    out_ref[...] = x_ref[...] + jnp.dot(lhs_ref[...], w_ref[...], preferred_element_type=F32)


def _out_proj(x2d, oa, ob, ocs, lses, w):
    m, d = x2d.shape
    k = w.shape[0]
    tm = _largest_tile(m, (1024, 512, 256))
    tn = _largest_tile(d, (512, 256, 128))
    row = lambda i, j: (i, 0)
    lhs = [oa, ob, *ocs, *lses]
    assert sum(a.shape[1] for a in (oa, ob, *ocs)) == k
    return pl.pallas_call(
        _out_proj_kernel,
        grid=(m // tm, d // tn),
        in_specs=[pl.BlockSpec((tm, a.shape[1]), row) for a in lhs]
        + [pl.BlockSpec((tm, tn), lambda i, j: (i, j)), pl.BlockSpec((k, tn), lambda i, j: (0, j))],
        out_specs=pl.BlockSpec((tm, tn), lambda i, j: (i, j)),
        out_shape=jax.ShapeDtypeStruct((m, d), F32),
        scratch_shapes=[pltpu.VMEM((tm, k), BF16)],
        compiler_params=_params("parallel", "arbitrary"),
        name="out_proj",
    )(*lhs, x2d, w)


def _cross_kernel(x_ref, g_ref, wq_ref, kv_ref, wo_ref, out_ref):
    x = x_ref[...]
    xn = _rms_normed(x, g_ref[...]).astype(BF16)
    q = (jnp.dot(xn, wq_ref[...], preferred_element_type=F32) * SCALE).astype(BF16)
    xw = X_HEADS * HEAD_DIM
    heads = []
    for h in range(X_HEADS):
        sl = slice(h * HEAD_DIM, (h + 1) * HEAD_DIM)
        k = kv_ref[:, sl]
        v = kv_ref[:, xw + h * HEAD_DIM:xw + (h + 1) * HEAD_DIM]
        s = lax.dot_general(q[:, sl], k, _NT, preferred_element_type=F32)
        e = jnp.exp(s - jnp.max(s, axis=-1, keepdims=True))
        den = jnp.sum(e, axis=-1, keepdims=True)
        heads.append((jnp.dot(e.astype(BF16), v, preferred_element_type=F32) / den).astype(BF16))
    o = jnp.concatenate(heads, axis=1)
    out_ref[...] = x + jnp.dot(o, wo_ref[...], preferred_element_type=F32)


def _cross(x2d, gain, wq, kv, wo, *, seq_len):
    m, d = x2d.shape
    tm = _largest_tile(math.gcd(m, seq_len), (512, 256))
    tiles_per_seq = seq_len // tm
    mem_len, kvw = kv.shape[1], kv.shape[2]
    return pl.pallas_call(
        _cross_kernel,
        grid=(m // tm,),
        in_specs=[pl.BlockSpec((tm, d), lambda i: (i, 0)),
                  pl.BlockSpec((1, d), lambda i: (0, 0)),
                  pl.BlockSpec(wq.shape, lambda i: (0, 0)),
                  pl.BlockSpec((None, mem_len, kvw), lambda i: (i // tiles_per_seq, 0, 0)),
                  pl.BlockSpec(wo.shape, lambda i: (0, 0))],
        out_specs=pl.BlockSpec((tm, d), lambda i: (i, 0)),
        out_shape=jax.ShapeDtypeStruct((m, d), F32),
        compiler_params=_params("parallel"),
        name="cross",
    )(x2d, gain.reshape(1, d), wq, kv, wo)


def _ffn_kernel(*refs, final_norm):
    if final_norm:
        x_ref, g_ref, wg_ref, wu_ref, wo_ref, gf_ref, out_ref, xn_ref = refs
    else:
        x_ref, g_ref, wg_ref, wu_ref, wo_ref, out_ref, xn_ref = refs
    c = pl.program_id(1)

    @pl.when(c == 0)
    def _():
        x = x_ref[...]
        xn_ref[...] = _rms_normed(x, g_ref[...]).astype(BF16)
        out_ref[...] = x

    h = xn_ref[...]
    gate = jnp.dot(h, wg_ref[...], preferred_element_type=F32)
    up = jnp.dot(h, wu_ref[...], preferred_element_type=F32)
    act = (gate * jax.nn.sigmoid(gate) * up).astype(BF16)
    out_ref[...] += jnp.dot(act, wo_ref[...], preferred_element_type=F32)

    if final_norm:
        @pl.when(c == pl.num_programs(1) - 1)
        def _():
            out_ref[...] = _rms_normed(out_ref[...], gf_ref[...])


def _ffn(x2d, gain, w_in, w_out, final_gain=None):
    m, d = x2d.shape
    d_ff = w_out.shape[0]
    assert w_in.shape == (d, 2 * d_ff)
    tm = _largest_tile(m, (512, 256))
    tf = _largest_tile(d_ff, (512, 256, 128))
    n_chunks = d_ff // tf
    in_specs = [pl.BlockSpec((tm, d), lambda i, c: (i, 0)),
                pl.BlockSpec((1, d), lambda i, c: (0, 0)),
                pl.BlockSpec((d, tf), lambda i, c: (0, c)),
                pl.BlockSpec((d, tf), lambda i, c: (0, c + n_chunks)),
                pl.BlockSpec((tf, d), lambda i, c: (c, 0))]
    args = [x2d, gain.reshape(1, d), w_in, w_in, w_out]
    if final_gain is not None:
        in_specs.append(pl.BlockSpec((1, d), lambda i, c: (0, 0)))
        args.append(final_gain.reshape(1, d))
    return pl.pallas_call(
        functools.partial(_ffn_kernel, final_norm=final_gain is not None),
        grid=(m // tm, n_chunks),
        in_specs=in_specs,
        out_specs=pl.BlockSpec((tm, d), lambda i, c: (i, 0)),
        out_shape=jax.ShapeDtypeStruct((m, d), F32),
        scratch_shapes=[pltpu.VMEM((tm, d), BF16)],
        compiler_params=_params("parallel", "arbitrary"),
        name="ffn",
    )(*args)


def _rope_tables(seq_len):
    rows = seq_len // GRID_W
    row = jnp.repeat(jnp.arange(rows), GRID_W).astype(F32)
    col = jnp.tile(jnp.arange(GRID_W), rows).astype(F32)
    inv = ROPE_THETA ** (-jnp.arange(0, ROPE_AXIS_DIM, 2, dtype=F32) / ROPE_AXIS_DIM)
    ang_r = row[:, None] * inv
    ang_c = col[:, None] * inv
    cos_t = jnp.concatenate([jnp.cos(ang_r)] * 2 + [jnp.cos(ang_c)] * 2, axis=-1)
    sin_t = jnp.concatenate([-jnp.sin(ang_r), jnp.sin(ang_r), -jnp.sin(ang_c), jnp.sin(ang_c)], axis=-1)
    return cos_t, sin_t


def _t5_bucket(rel):
    nb = REL_BUCKETS // 2
    max_exact = nb // 2
    ret = jnp.where(rel > 0, nb, 0)
    n = jnp.abs(rel)
    large = max_exact + (jnp.log(jnp.maximum(n, 1).astype(F32) / max_exact)
                         / math.log(REL_MAX_DIST / max_exact) * (nb - max_exact)).astype(jnp.int32)
    large = jnp.minimum(large, nb - 1)
    return ret + jnp.where(n < max_exact, n, large)


def _band_bias(rel_bias, head0, kv_heads, group, blk, dil):
    off = (jnp.arange(3 * blk) - blk)[None, :] - jnp.arange(blk)[:, None]
    table = rel_bias[:, head0:head0 + kv_heads * group].astype(F32)
    bias = jnp.moveaxis(table[_t5_bucket(off * dil)], -1, 0)
    bias = jnp.where((jnp.abs(off) <= blk)[None], bias, NEG_INF)
    masked = jnp.full((kv_heads * group, blk, blk), NEG_INF, F32)
    first = jnp.concatenate([bias[:, :, blk:], masked], axis=-1)
    last = jnp.concatenate([masked, bias[:, :, :2 * blk]], axis=-1)
    variants = jnp.stack([bias, first, last], axis=1)
    variants = variants.reshape(kv_heads, group, 3, blk, 3 * blk)
    return jnp.moveaxis(variants, 1, 2).reshape(kv_heads, 3, group * blk, 3 * blk)


def _encode(x, mem, ln_mix, w_in, q_norm_a, k_norm_a, sink_b, w_out, ln_cross, ln_mem, w_cq, w_ckv, w_co,
            ln_ffn, w_ffn_in, w_ffn_out, ln_final, bias_b, bias_c):
    b, s, d = x.shape
    mem_len = mem.shape[1]
    depth = w_in.shape[0]
    cos_t, sin_t = _rope_tables(s)
    x2d = x.reshape(b * s, d)
    mem2d = mem.reshape(b * mem_len, d)
    b_group = B_HEADS // B_KV_HEADS
    for l in range(depth):
        gains = jnp.stack([q_norm_a[l], k_norm_a[l]]).astype(F32)
        qkv = _proj(x2d, ln_mix[l], w_in[l], _HEAD_TYPES, seq_len=s, rope=(gains, cos_t, sin_t), tn_heads=10)
        qkv = qkv.reshape(b, s, PROJ_HEADS * HEAD_DIM)
        out_a = _attn_a(qkv)
        sink_col = jnp.repeat(sink_b[l].astype(F32).reshape(B_KV_HEADS, b_group), B_WINDOW, axis=1)[..., None]
        (out_b,) = _banded(qkv, bias_b, sink_col, dil=1, blk=B_WINDOW, group=b_group, kv_heads=B_KV_HEADS,
                           q_head0=_QB, k_head0=_KB, v_head0=_VB, emit_lse=False)
        ocs, lses = [], []
        for gi, (window, dil) in enumerate(C_PATTERNS):
            q0, k0, v0 = _QC + gi * C_HEADS_PER_GROUP, _KC + gi, _VC + gi
            src = qkv
            if dil > 1:
                cols = lambda h0, n: qkv[:, :, h0 * HEAD_DIM:(h0 + n) * HEAD_DIM]
                src = jnp.concatenate([cols(q0, C_HEADS_PER_GROUP), cols(k0, 1), cols(v0, 1)], axis=-1)
                q0, k0, v0 = 0, C_HEADS_PER_GROUP, C_HEADS_PER_GROUP + 1
            o_g, lse_g = _banded(src, bias_c[gi], None, dil=dil, blk=window // (2 * dil), group=C_HEADS_PER_GROUP,
                                 kv_heads=1, q_head0=q0, k_head0=k0, v_head0=v0, emit_lse=True)
            ocs.append(o_g.reshape(b * s, -1))
            lses.append(lse_g.reshape(b * s, -1))
        x2d = _out_proj(x2d, out_a.reshape(b * s, -1), out_b.reshape(b * s, -1), ocs, lses, w_out[l])
        kv = _proj(mem2d, ln_mem[l], w_ckv[l], ("plain",) * (2 * X_HEADS), seq_len=mem_len, tn_heads=2 * X_HEADS)
        x2d = _cross(x2d, ln_cross[l], w_cq[l], kv.reshape(b, mem_len, -1), w_co[l], seq_len=s)
        x2d = _ffn(x2d, ln_ffn[l], w_ffn_in[l], w_ffn_out[l], ln_final if l == depth - 1 else None)
    return x2d.reshape(b, s, d)


def kernel(x_prompt, x_sample, mem_prompt, mem_sample, ln_mix, w_in, q_norm_a, k_norm_a, sink_b, rel_bias, w_out,
           ln_cross, ln_mem, w_cq, w_ckv, w_co, ln_ffn, w_ffn_in, w_ffn_out, ln_final):
    bias_b = _band_bias(rel_bias, 0, B_KV_HEADS, B_HEADS // B_KV_HEADS, B_WINDOW, 1)
    bias_c = [_band_bias(rel_bias, B_HEADS + gi * C_HEADS_PER_GROUP, 1, C_HEADS_PER_GROUP, w // (2 * dl), dl)
              for gi, (w, dl) in enumerate(C_PATTERNS)]
    weights = [w.astype(BF16) for w in (w_in, w_out, w_cq, w_ckv, w_co, w_ffn_in, w_ffn_out)]
    w_in_b, w_out_b, w_cq_b, w_ckv_b, w_co_b, w_ffn_in_b, w_ffn_out_b = weights

    def run(x, mem):
        return _encode(x, mem, ln_mix, w_in_b, q_norm_a, k_norm_a, sink_b, w_out_b, ln_cross, ln_mem, w_cq_b,
                       w_ckv_b, w_co_b, ln_ffn, w_ffn_in_b, w_ffn_out_b, ln_final, bias_b, bias_c)

    return (run(x_prompt, mem_prompt), run(x_sample, mem_sample))
```

```python
import functools
import math

import jax
import jax.numpy as jnp
import numpy as np
from jax import lax
from jax.experimental import pallas as pl
from jax.experimental.pallas import tpu as pltpu

HEAD_DIM = 128
SCALE = HEAD_DIM ** -0.5
A_HEADS, A_KV_HEADS = 6, 2
B_HEADS, B_KV_HEADS = 4, 2
B_WINDOW = 128
C_PATTERNS = ((128, 1), (512, 4), (2048, 16))
C_HEADS_PER_GROUP = 2
C_GROUPS = len(C_PATTERNS)
GRID_W = 64
ROPE_THETA = 10000.0
ROPE_AXIS_DIM = HEAD_DIM // 2
REL_BUCKETS = 32
REL_MAX_DIST = 1024
X_HEADS = 4
RMS_EPS = 1e-6
NEG_INF = -1e30
LOG2_E = math.log2(math.e)
BANDED_UNROLL = 4

_QA, _KA, _VA = 0, A_HEADS, A_HEADS + A_KV_HEADS
_QB = _VA + A_KV_HEADS
_KB, _VB = _QB + B_HEADS, _QB + B_HEADS + B_KV_HEADS
_QC = _VB + B_KV_HEADS
_KC = _QC + C_GROUPS * C_HEADS_PER_GROUP
_VC = _KC + C_GROUPS
PROJ_HEADS = _VC + C_GROUPS
_HEAD_TYPES = (("rope_q",) * A_HEADS + ("rope_k",) * A_KV_HEADS + ("plain",) * A_KV_HEADS
               + ("scale",) * B_HEADS + ("plain",) * (2 * B_KV_HEADS)
               + ("scale",) * (C_GROUPS * C_HEADS_PER_GROUP) + ("plain",) * (2 * C_GROUPS))
MIX_HEADS = A_HEADS + B_HEADS + C_GROUPS * C_HEADS_PER_GROUP

V7X_VMEM_LIMIT_BYTES = 56 * 1024 * 1024

BF16 = jnp.bfloat16
F32 = jnp.float32
_NT = (((1,), (1,)), ((), ()))


def _largest_tile(n, candidates):
    for c in candidates:
        if n % c == 0:
            return c
    raise ValueError(f"no tile in {candidates} divides {n}")


def _params(*sem):
    return pltpu.CompilerParams(dimension_semantics=sem, vmem_limit_bytes=V7X_VMEM_LIMIT_BYTES)


def _rms_normed(x, gain):
    return x * lax.rsqrt(jnp.mean(x * x, axis=-1, keepdims=True) + RMS_EPS) * gain


def _proj_kernel(*refs, head_types, heads_per_block):
    has_rope = any(t.startswith("rope") for t in head_types)
    if has_rope:
        x_ref, g_ref, w_ref, gains_ref, cos_ref, sin_ref, o_ref, xn_ref = refs
    else:
        x_ref, g_ref, w_ref, o_ref, xn_ref = refs
    j = pl.program_id(1)

    @pl.when(j == 0)
    def _():
        xn_ref[...] = _rms_normed(x_ref[...], g_ref[...]).astype(BF16)

    acc = jnp.dot(xn_ref[...], w_ref[...], preferred_element_type=F32)
    tm = acc.shape[0]

    def rope_head(y, gain, post):
        y = _rms_normed(y, gain)
        lane = lax.broadcasted_iota(jnp.int32, (tm, HEAD_DIM), 1)
        first_quarter = (lane % (HEAD_DIM // 2)) < (HEAD_DIM // 4)
        partner = jnp.where(first_quarter, pltpu.roll(y, 3 * HEAD_DIM // 4, 1), pltpu.roll(y, HEAD_DIM // 4, 1))
        out = y * cos_ref[...] + partner * sin_ref[...]
        return out * post if post != 1.0 else out

    def epilogue(types):
        h = 0
        while h < len(types):
            ty = types[h]
            if ty in ("rope_q", "rope_k"):
                sl = slice(h * HEAD_DIM, (h + 1) * HEAD_DIM)
                row = 0 if ty == "rope_q" else 1
                post = SCALE * LOG2_E if ty == "rope_q" else 1.0
                o_ref[:, sl] = rope_head(acc[:, sl], gains_ref[row:row + 1, :], post).astype(BF16)
                h += 1
            else:
                e = h
                while e < len(types) and types[e] == ty:
                    e += 1
                sl = slice(h * HEAD_DIM, e * HEAD_DIM)
                val = acc[:, sl] * SCALE if ty == "scale" else acc[:, sl]
                o_ref[:, sl] = val.astype(BF16)
                h = e

    n_blocks = len(head_types) // heads_per_block
    if n_blocks == 1:
        epilogue(head_types)
    else:
        for jb in range(n_blocks):
            pl.when(j == jb)(functools.partial(
                epilogue, head_types[jb * heads_per_block:(jb + 1) * heads_per_block]))


def _proj(x2d, gain, w, head_types, *, seq_len, rope=None, tn_heads):
    m, d = x2d.shape
    n = w.shape[1]
    assert n == len(head_types) * HEAD_DIM and len(head_types) % tn_heads == 0
    tm = _largest_tile(math.gcd(m, seq_len), (1024, 512, 256))
    tn = tn_heads * HEAD_DIM
    in_specs = [pl.BlockSpec((tm, d), lambda i, j: (i, 0)),
                pl.BlockSpec((1, d), lambda i, j: (0, 0)),
                pl.BlockSpec((d, tn), lambda i, j: (0, j))]
    args = [x2d, gain.reshape(1, d), w]
    if rope is not None:
        gains, cos_t, sin_t = rope
        tiles_per_seq = seq_len // tm
        in_specs += [pl.BlockSpec((2, HEAD_DIM), lambda i, j: (0, 0)),
                     pl.BlockSpec((tm, HEAD_DIM), lambda i, j: (i % tiles_per_seq, 0)),
                     pl.BlockSpec((tm, HEAD_DIM), lambda i, j: (i % tiles_per_seq, 0))]
        args += [gains, cos_t, sin_t]
    return pl.pallas_call(
        functools.partial(_proj_kernel, head_types=tuple(head_types), heads_per_block=tn_heads),
        grid=(m // tm, n // tn),
        in_specs=in_specs,
        out_specs=pl.BlockSpec((tm, tn), lambda i, j: (i, j)),
        out_shape=jax.ShapeDtypeStruct((m, n), BF16),
        scratch_shapes=[pltpu.VMEM((tm, d), BF16)],
        compiler_params=_params("parallel", "arbitrary"),
        name="proj",
    )(*args)


def _attn_a_kernel(q_ref, k_ref, v_ref, o_ref, q_scr, s0_scr, s1_scr, m_scr, l_scr, acc_scr, *, group, tk):
    tq = q_ref.shape[0]
    n_chunks = k_ref.shape[0] // tk
    n_slabs = tk // HEAD_DIM
    for h in range(group):
        q_scr[h * tq:(h + 1) * tq, :] = q_ref[:, h * HEAD_DIM:(h + 1) * HEAD_DIM]
    m_scr[...] = jnp.full(m_scr.shape, -jnp.inf, F32)
    l_scr[...] = jnp.zeros(l_scr.shape, F32)
    acc_scr[...] = jnp.zeros(acc_scr.shape, F32)

    def scores(c, s_scr):
        off = pl.multiple_of(c * tk, tk)
        s_scr[...] = lax.dot_general(q_scr[...], k_ref[pl.ds(off, tk), :], _NT, preferred_element_type=F32)

    def accumulate(c, s_scr):
        off = pl.multiple_of(c * tk, tk)
        slabs = [s_scr[:, j * HEAD_DIM:(j + 1) * HEAD_DIM] for j in range(n_slabs)]
        m_prev = m_scr[...]
        m_new = jnp.maximum(m_prev, jnp.max(functools.reduce(jnp.maximum, slabs), axis=-1, keepdims=True))
        a = jnp.exp2(m_prev - m_new)
        ps = [jnp.exp2(s - m_new) for s in slabs]
        l_scr[...] = a * l_scr[...] + functools.reduce(jnp.add, ps)
        p = jnp.concatenate([p.astype(BF16) for p in ps], axis=1)
        acc_scr[...] = a * acc_scr[...] + jnp.dot(p, v_ref[pl.ds(off, tk), :], preferred_element_type=F32)
        m_scr[...] = m_new

    scores(0, s0_scr)

    def pair(cc, carry):
        c = 2 * cc
        scores(c + 1, s1_scr)
        accumulate(c, s0_scr)
        scores(jnp.minimum(c + 2, n_chunks - 1), s0_scr)
        accumulate(c + 1, s1_scr)
        return carry

    lax.fori_loop(0, n_chunks // 2, pair, 0)
    out =acc_scr[...] / jnp.sum(l_scr[...], axis=-1, keepdims=True)
    for h in range(group):
        o_ref[:, h * HEAD_DIM:(h + 1) * HEAD_DIM] = out[h * tq:(h + 1) * tq, :].astype(BF16)


def _attn_a(qkv, *, tq=512, tk=512):
    b, s, _ = qkv.shape
    group = A_HEADS // A_KV_HEADS
    gw = group * HEAD_DIM
    assert s % tq == 0 and s % (2 * tk) == 0 and _QA == 0
    return pl.pallas_call(
        functools.partial(_attn_a_kernel, group=group, tk=tk),
        grid=(b, A_KV_HEADS, s // tq),
        in_specs=[pl.BlockSpec((None, tq, gw), lambda bi, kv, i: (bi, i, kv)),
                  pl.BlockSpec((None, s, HEAD_DIM), lambda bi, kv, i: (bi, 0, _KA + kv)),
                  pl.BlockSpec((None, s, HEAD_DIM), lambda bi, kv, i: (bi, 0, _VA + kv))],
        out_specs=pl.BlockSpec((None, tq, gw), lambda bi, kv, i: (bi, i, kv)),
        out_shape=jax.ShapeDtypeStruct((b, s, A_HEADS * HEAD_DIM), BF16),
        scratch_shapes=[pltpu.VMEM((group * tq, HEAD_DIM), BF16),
                        pltpu.VMEM((group * tq, tk), F32),
                        pltpu.VMEM((group * tq, tk), F32),
                        pltpu.VMEM((group * tq, HEAD_DIM), F32),
                        pltpu.VMEM((group * tq, HEAD_DIM), F32),
                        pltpu.VMEM((group * tq, HEAD_DIM), F32)],
        compiler_params=_params("parallel", "parallel", "arbitrary"),
        name="attn_a",
    )(qkv, qkv, qkv)


def _banded_kernel(*refs, blk, group, has_sink, emit_lse):
    q_ref, k_ref, v_ref, bias_ref = refs[:4]
    refs = refs[4:]
    sink_ref = None
    if has_sink:
        sink_ref, refs = refs[0], refs[1:]
    o_ref = refs[0]
    lse_ref = refs[1] if emit_lse else None
    tl = q_ref.shape[0]
    n_blocks = k_ref.shape[0] // blk
    blocks_per_tile = tl // blk
    t = pl.program_id(3)

    def block(i, carry):
        n = t * blocks_per_tile + i
        row0 = pl.multiple_of(i * blk, blk)
        start = pl.multiple_of(jnp.clip(n - 1, 0, n_blocks - 3) * blk, blk)
        variant = jnp.where(n == 0, 1, jnp.where(n == n_blocks - 1, 2, 0))
        q = jnp.concatenate([q_ref[pl.ds(row0, blk), h * HEAD_DIM:(h + 1) * HEAD_DIM] for h in range(group)], axis=0)
        kw = k_ref[pl.ds(start, 3 * blk), :]
        vw = v_ref[pl.ds(start, 3 * blk), :]
        s = lax.dot_general(q, kw, _NT, preferred_element_type=F32) + bias_ref[variant]
        m = jnp.max(s, axis=-1, keepdims=True)
        if has_sink:
            sk = sink_ref[...]
            m = jnp.maximum(m, sk)
        e = jnp.exp(s - m)
        den = jnp.sum(e, axis=-1, keepdims=True)
        if has_sink:
            den = den + jnp.exp(sk - m)
        o = jnp.dot(e.astype(BF16), vw, preferred_element_type=F32) / den
        for h in range(group):
            o_ref[pl.ds(row0, blk), h * HEAD_DIM:(h + 1) * HEAD_DIM] = o[h * blk:(h + 1) * blk, :].astype(BF16)
        if emit_lse:
            lse = m + jnp.log(den)
            for h in range(group):
                lse_ref[pl.ds(row0, blk), h * HEAD_DIM:(h + 1) * HEAD_DIM] = jnp.broadcast_to(
                    lse[h * blk:(h + 1) * blk, :], (blk, HEAD_DIM))
        return carry

    lax.fori_loop(0, blocks_per_tile, block, 0, unroll=BANDED_UNROLL)


def _banded(qkv, bias, sink_col, *, dil, blk, group, kv_heads, q_head0, k_head0, v_head0, emit_lse):
    b, s, width = qkv.shape
    heads_per_row = width // HEAD_DIM
    length = s // dil
    assert s % dil == 0 and length % blk == 0 and length // blk >= 3
    tl = _largest_tile(length, (2048, 1024, 512, 256))
    gw = group * HEAD_DIM
    assert q_head0 % group == 0 and heads_per_row % group == 0
    qkv_r = qkv.reshape(b, length, dil * width)
    in_specs = [
        pl.BlockSpec((None, tl, gw), lambda bi, r, kv, t: (bi, t, (r * heads_per_row + q_head0) // group + kv)),
        pl.BlockSpec((None, length, HEAD_DIM), lambda bi, r, kv, t: (bi, 0, r * heads_per_row + k_head0 + kv)),
        pl.BlockSpec((None, length, HEAD_DIM), lambda bi, r, kv, t: (bi, 0, r * heads_per_row + v_head0 + kv)),
        pl.BlockSpec((None, 3, group * blk, 3 * blk), lambda bi, r, kv, t: (kv, 0, 0, 0)),
    ]
    args = [qkv_r, qkv_r, qkv_r, bias]
    if sink_col is not None:
        in_specs.append(pl.BlockSpec((None, group * blk, 1), lambda bi, r, kv, t: (kv, 0, 0)))
        args.append(sink_col)
    out_w = kv_heads * gw
    o_spec = pl.BlockSpec((None, tl, gw), lambda bi, r, kv, t: (bi, t, r * kv_heads + kv))
    out_shape = [jax.ShapeDtypeStruct((b, length, dil * out_w), BF16)]
    out_specs = [o_spec]
    if emit_lse:
        out_shape.append(jax.ShapeDtypeStruct((b, length, dil * out_w), F32))
        out_specs.append(o_spec)
    outs = pl.pallas_call(
        functools.partial(_banded_kernel, blk=blk, group=group, has_sink=sink_col is not None, emit_lse=emit_lse),
        grid=(b, dil, kv_heads, length // tl),
        in_specs=in_specs,
        out_specs=out_specs,
        out_shape=out_shape,
        compiler_params=_params("parallel", "parallel", "parallel", "arbitrary"),
        name=f"banded_d{dil}",
    )(*args)
    return [o.reshape(b, s, out_w) for o in outs]


def _out_proj_kernel(oa_ref, ob_ref, oc0_ref, oc1_ref, oc2_ref, l0_ref, l1_ref, l2_ref, x_ref, w_ref,
                     out_ref, lhs_ref):
    j = pl.program_id(1)

    @pl.when(j == 0)
    def _():
        a_w = oa_ref.shape[1]
        b_w = ob_ref.shape[1]
        c_w = oc0_ref.shape[1]
        lhs_ref[:, 0:a_w] = oa_ref[...]
        lhs_ref[:, a_w:a_w + b_w] = ob_ref[...]
        ls = [l0_ref[...], l1_ref[...], l2_ref[...]]
        mx = jnp.maximum(jnp.maximum(ls[0], ls[1]), ls[2])
        es = [jnp.exp(l - mx) for l in ls]
        tot = es[0] + es[1] + es[2]
        for gi, oc_ref in enumerate((oc0_ref, oc1_ref, oc2_ref)):
            c0 = a_w + b_w + gi * c_w
            lhs_ref[:, c0:c0 + c_w] = (oc_ref[...].astype(F32) * (es[gi] / tot)).astype(BF16)

    out_ref[...] = x_ref[...] + jnp.dot(lhs_ref[...], w_ref[...], preferred_element_type=F32)


def _out_proj(x2d, oa, ob, ocs, lses, w):
    m, d = x2d.shape
    k = w.shape[0]
    tm = _largest_tile(m, (512, 256))
    tn = d
    row = lambda i, j: (i, 0)
    lhs = [oa, ob, *ocs, *lses]
    assert sum(a.shape[1] for a in (oa, ob, *ocs)) == k
    return pl.pallas_call(
        _out_proj_kernel,
        grid=(m // tm, d // tn),
        in_specs=[pl.BlockSpec((tm, a.shape[1]), row) for a in lhs]
        + [pl.BlockSpec((tm, tn), lambda i, j: (i, j)), pl.BlockSpec((k, tn), lambda i, j: (0, j))],
        out_specs=pl.BlockSpec((tm, tn), lambda i, j: (i, j)),
        out_shape=jax.ShapeDtypeStruct((m, d), F32),
        scratch_shapes=[pltpu.VMEM((tm, k), BF16)],
        compiler_params=_params("parallel", "arbitrary"),
        name="out_proj",
    )(*lhs, x2d, w)


def _cross_kernel(x_ref, g_ref, wq_ref, kv_ref, wo_ref, out_ref):
    x = x_ref[...]
    xn = _rms_normed(x, g_ref[...]).astype(BF16)
    q = (jnp.dot(xn, wq_ref[...], preferred_element_type=F32) * SCALE).astype(BF16)
    xw = X_HEADS * HEAD_DIM
    heads = []
    for h in range(X_HEADS):
        sl = slice(h * HEAD_DIM, (h + 1) * HEAD_DIM)
        k = kv_ref[:, sl]
        v = kv_ref[:, xw + h * HEAD_DIM:xw + (h + 1) * HEAD_DIM]
        s = lax.dot_general(q[:, sl], k, _NT, preferred_element_type=F32)
        e = jnp.exp(s - jnp.max(s, axis=-1, keepdims=True))
        den = jnp.sum(e, axis=-1, keepdims=True)
        heads.append((jnp.dot(e.astype(BF16), v, preferred_element_type=F32) / den).astype(BF16))
    o = jnp.concatenate(heads, axis=1)
    out_ref[...] = x + jnp.dot(o, wo_ref[...], preferred_element_type=F32)


def _cross(x2d, gain, wq, kv, wo, *, seq_len):
    m, d = x2d.shape
    tm = _largest_tile(math.gcd(m, seq_len), (512, 256))
    tiles_per_seq = seq_len // tm
    mem_len, kvw = kv.shape[1], kv.shape[2]
    return pl.pallas_call(
        _cross_kernel,
        grid=(m // tm,),
        in_specs=[pl.BlockSpec((tm, d), lambda i: (i, 0)),
                  pl.BlockSpec((1, d), lambda i: (0, 0)),
                  pl.BlockSpec(wq.shape, lambda i: (0, 0)),
                  pl.BlockSpec((None, mem_len, kvw), lambda i: (i // tiles_per_seq, 0, 0)),
                  pl.BlockSpec(wo.shape, lambda i: (0, 0))],
        out_specs=pl.BlockSpec((tm, d), lambda i: (i, 0)),
        out_shape=jax.ShapeDtypeStruct((m, d), F32),
        compiler_params=_params("parallel"),
        name="cross",
    )(x2d, gain.reshape(1, d), wq, kv, wo)


def _ffn_kernel(*refs, final_norm):
    if final_norm:
        x_ref, g_ref, wg_ref, wu_ref, wo_ref, gf_ref, out_ref, xn_ref = refs
    else:
        x_ref, g_ref, wg_ref, wu_ref, wo_ref, out_ref, xn_ref = refs
    c = pl.program_id(1)

    @pl.when(c == 0)
    def _():
        x = x_ref[...]
        xn_ref[...] = _rms_normed(x, g_ref[...]).astype(BF16)
        out_ref[...] = x

    h = xn_ref[...]
    gate = jnp.dot(h, wg_ref[...], preferred_element_type=F32)
    up = jnp.dot(h, wu_ref[...], preferred_element_type=F32)
    act = (gate * jax.nn.sigmoid(gate) * up).astype(BF16)
    out_ref[...] += jnp.dot(act, wo_ref[...], preferred_element_type=F32)

    if final_norm:
        @pl.when(c == pl.num_programs(1) - 1)
        def _():
            out_ref[...] = _rms_normed(out_ref[...], gf_ref[...])


def _ffn(x2d, gain, w_in, w_out, final_gain=None):
    m, d = x2d.shape
    d_ff = w_out.shape[0]
    assert w_in.shape == (d, 2 * d_ff)
    tm = _largest_tile(m, (512, 256))
    tf = _largest_tile(d_ff, (512, 256, 128))
    n_chunks = d_ff // tf
    in_specs = [pl.BlockSpec((tm, d), lambda i, c: (i, 0)),
                pl.BlockSpec((1, d), lambda i, c: (0, 0)),
                pl.BlockSpec((d, tf), lambda i, c: (0, c)),
                pl.BlockSpec((d, tf), lambda i, c: (0, c + n_chunks)),
                pl.BlockSpec((tf, d), lambda i, c: (c, 0))]
    args = [x2d, gain.reshape(1, d), w_in, w_in, w_out]
    if final_gain is not None:
        in_specs.append(pl.BlockSpec((1, d), lambda i, c: (0, 0)))
        args.append(final_gain.reshape(1, d))
    return pl.pallas_call(
        functools.partial(_ffn_kernel, final_norm=final_gain is not None),
        grid=(m // tm, n_chunks),
        in_specs=in_specs,
        out_specs=pl.BlockSpec((tm, d), lambda i, c: (i, 0)),
        out_shape=jax.ShapeDtypeStruct((m, d), F32),
        scratch_shapes=[pltpu.VMEM((tm, d), BF16)],
        compiler_params=_params("parallel", "arbitrary"),
        name="ffn",
    )(*args)


def _rope_tables(seq_len):
    rows = seq_len // GRID_W
    row = jnp.repeat(jnp.arange(rows), GRID_W).astype(F32)
    col = jnp.tile(jnp.arange(GRID_W), rows).astype(F32)
    inv = ROPE_THETA ** (-jnp.arange(0, ROPE_AXIS_DIM, 2, dtype=F32) / ROPE_AXIS_DIM)
    ang_r = row[:, None] * inv
    ang_c = col[:, None] * inv
    cos_t = jnp.concatenate([jnp.cos(ang_r)] * 2 + [jnp.cos(ang_c)] * 2, axis=-1)
    sin_t = jnp.concatenate([-jnp.sin(ang_r), jnp.sin(ang_r), -jnp.sin(ang_c), jnp.sin(ang_c)], axis=-1)
    return cos_t, sin_t


def _t5_bucket(rel):
    nb = REL_BUCKETS // 2
    max_exact = nb // 2
    ret = jnp.where(rel > 0, nb, 0)
    n = jnp.abs(rel)
    large = max_exact + (jnp.log(jnp.maximum(n, 1).astype(F32) / max_exact)
                         / math.log(REL_MAX_DIST / max_exact) * (nb - max_exact)).astype(jnp.int32)
    large = jnp.minimum(large, nb - 1)
    return ret + jnp.where(n < max_exact, n, large)


def _band_bias(rel_bias, head0, kv_heads, group, blk, dil):
    off = (jnp.arange(3 * blk) - blk)[None, :] - jnp.arange(blk)[:, None]
    table = rel_bias[:, head0:head0 + kv_heads * group].astype(F32)
    bias = jnp.moveaxis(table[_t5_bucket(off * dil)], -1, 0)
    bias = jnp.where((jnp.abs(off) <= blk)[None], bias, NEG_INF)
    masked = jnp.full((kv_heads * group, blk, blk), NEG_INF, F32)
    first = jnp.concatenate([bias[:, :, blk:], masked], axis=-1)
    last = jnp.concatenate([masked, bias[:, :, :2 * blk]], axis=-1)
    variants = jnp.stack([bias, first, last], axis=1)
    variants = variants.reshape(kv_heads, group, 3, blk, 3 * blk)
    return jnp.moveaxis(variants, 1, 2).reshape(kv_heads, 3, group * blk, 3 * blk)


def _encode(x, mem, ln_mix, w_in, q_norm_a, k_norm_a, sink_b, w_out, ln_cross, ln_mem, w_cq, w_ckv, w_co,
            ln_ffn, w_ffn_in, w_ffn_out, ln_final, bias_b, bias_c):
    b, s, d = x.shape
    mem_len = mem.shape[1]
    depth = w_in.shape[0]
    cos_t, sin_t = _rope_tables(s)
    x2d = x.reshape(b * s, d)
    mem2d = mem.reshape(b * mem_len, d)
    b_group = B_HEADS // B_KV_HEADS
    for l in range(depth):
        gains = jnp.stack([q_norm_a[l], k_norm_a[l]]).astype(F32)
        qkv = _proj(x2d, ln_mix[l], w_in[l], _HEAD_TYPES, seq_len=s, rope=(gains, cos_t, sin_t), tn_heads=10)
        qkv = qkv.reshape(b, s, PROJ_HEADS * HEAD_DIM)
        out_a = _attn_a(qkv)
        sink_col = jnp.repeat(sink_b[l].astype(F32).reshape(B_KV_HEADS, b_group), B_WINDOW, axis=1)[..., None]
        (out_b,) = _banded(qkv, bias_b, sink_col, dil=1, blk=B_WINDOW, group=b_group, kv_heads=B_KV_HEADS,
                           q_head0=_QB, k_head0=_KB, v_head0=_VB, emit_lse=False)
        ocs, lses = [], []
        for gi, (window, dil) in enumerate(C_PATTERNS):
            q0, k0, v0 = _QC + gi * C_HEADS_PER_GROUP, _KC + gi, _VC + gi
            src = qkv
            if dil > 1:
                cols = lambda h0, n: qkv[:, :, h0 * HEAD_DIM:(h0 + n) * HEAD_DIM]
                src = jnp.concatenate([cols(q0, C_HEADS_PER_GROUP), cols(k0, 1), cols(v0, 1)], axis=-1)
                q0, k0, v0 = 0, C_HEADS_PER_GROUP, C_HEADS_PER_GROUP + 1
            o_g, lse_g = _banded(src, bias_c[gi], None, dil=dil, blk=window // (2 * dil), group=C_HEADS_PER_GROUP,
                                 kv_heads=1, q_head0=q0, k_head0=k0, v_head0=v0, emit_lse=True)
            ocs.append(o_g.reshape(b * s, -1))
            lses.append(lse_g.reshape(b * s, -1))
        x2d = _out_proj(x2d, out_a.reshape(b * s, -1), out_b.reshape(b * s, -1), ocs, lses, w_out[l])
        kv = _proj(mem2d, ln_mem[l], w_ckv[l], ("plain",) * (2 * X_HEADS), seq_len=mem_len, tn_heads=2 * X_HEADS)
        x2d = _cross(x2d, ln_cross[l], w_cq[l], kv.reshape(b, mem_len, -1), w_co[l], seq_len=s)
        x2d = _ffn(x2d, ln_ffn[l], w_ffn_in[l], w_ffn_out[l], ln_final if l == depth - 1 else None)
    return x2d.reshape(b, s, d)


def kernel(x_prompt, x_sample, mem_prompt, mem_sample, ln_mix, w_in, q_norm_a, k_norm_a, sink_b, rel_bias, w_out,
           ln_cross, ln_mem, w_cq, w_ckv, w_co, ln_ffn, w_ffn_in, w_ffn_out, ln_final):
    bias_b = _band_bias(rel_bias, 0, B_KV_HEADS, B_HEADS // B_KV_HEADS, B_WINDOW, 1)
    bias_c = [_band_bias(rel_bias, B_HEADS + gi * C_HEADS_PER_GROUP, 1, C_HEADS_PER_GROUP, w // (2 * dl), dl)
              for gi, (w, dl) in enumerate(C_PATTERNS)]
    weights = [w.astype(BF16) for w in (w_in, w_out, w_cq, w_ckv, w_co, w_ffn_in, w_ffn_out)]
    w_in_b, w_out_b, w_cq_b, w_ckv_b, w_co_b, w_ffn_in_b, w_ffn_out_b = weights

    def run(x, mem):
        return _encode(x, mem, ln_mix, w_in_b, q_norm_a, k_norm_a, sink_b, w_out_b, ln_cross, ln_mem, w_cq_b,
                       w_ckv_b, w_co_b, ln_ffn, w_ffn_in_b, w_ffn_out_b, ln_final, bias_b, bias_c)

    return (run(x_prompt, mem_prompt), run(x_sample, mem_sample))
```
